```python
import jax, jax.numpy as jnp
from jax import lax
import numpy as np

D_MODEL = 1024
BATCH = 8
SEQ = 4096
DEPTH = 1

N_ATTN_HEADS = 8
HEAD_DIM = 64
ATTN_WIDTH = N_ATTN_HEADS * HEAD_DIM
SGU_WIDTH = D_MODEL - ATTN_WIDTH
N_SGU_GROUPS = 4
SGU_GROUP_DIM = SGU_WIDTH // N_SGU_GROUPS
IN_WIDTH = 3 * ATTN_WIDTH + 2 * SGU_WIDTH
MOBA_BLOCK = 256
MOBA_TOPK = 3
Q_CHUNK = 32
SGU_CHUNK = 128
D_FF = -(-8 * D_MODEL // (3 * 256)) * 256
N_MOD = 6
EPS = 1e-6
NEG = -1e30

kernel_name = "hymba_moba_gmlp_swiglu_adaln"


def rms_norm(x, g):
    xf = x.astype(jnp.float32)
    y = xf * lax.rsqrt(jnp.mean(xf * xf, axis=-1, keepdims=True) + EPS)
    return (y * g.astype(jnp.float32)).astype(x.dtype)


def layer_norm(x, g):
    xf = x.astype(jnp.float32)
    mu = jnp.mean(xf, axis=-1, keepdims=True)
    d = xf - mu
    y = d * lax.rsqrt(jnp.mean(d * d, axis=-1, keepdims=True) + EPS)
    return (y * g.astype(jnp.float32)).astype(x.dtype)


def modulate(h, shift, scale):
    return h * (1 + scale[:, None, :]) + shift[:, None, :]


def gather_blocks(blocks, idx):
    return jax.vmap(jax.vmap(lambda blk, i: blk[i]))(blocks, idx)


def moba_attention(q, k, v):
    B, H, S, Dh = q.shape
    S_pad = -(-S // MOBA_BLOCK) * MOBA_BLOCK
    pad = S_pad - S
    if pad:
        cfg = ((0, 0), (0, 0), (0, pad), (0, 0))
        q, k, v = jnp.pad(q, cfg), jnp.pad(k, cfg), jnp.pad(v, cfg)
    nb = S_pad // MOBA_BLOCK
    k_sel = max(1, min(MOBA_TOPK, nb - 1))
    kb = k.reshape(B, H, nb, MOBA_BLOCK, Dh)
    vb = v.reshape(B, H, nb, MOBA_BLOCK, Dh)
    k_mean = jnp.mean(kb.astype(jnp.float32), axis=3)
    q_blk = jnp.arange(S_pad) // MOBA_BLOCK
    gate = jnp.einsum('bhtd,bhnd->bhtn', q.astype(jnp.float32), k_mean)
    fully_past = jnp.arange(nb)[None, :] < q_blk[:, None]
    gate = jnp.where(fully_past, gate, -jnp.inf)
    _, sel = lax.top_k(gate, k_sel)
    sel_valid = sel < q_blk[:, None]
    scale = Dh ** -0.5

    def chunk_fn(ci):
        t0 = ci * Q_CHUNK
        qc = lax.dynamic_slice_in_dim(q, t0, Q_CHUNK, axis=2)
        selc = lax.dynamic_slice_in_dim(sel, t0, Q_CHUNK, axis=2)
        validc = lax.dynamic_slice_in_dim(sel_valid, t0, Q_CHUNK, axis=2)
        blk = t0 // MOBA_BLOCK
        k_own = lax.dynamic_index_in_dim(kb, blk, axis=2, keepdims=False)
        v_own = lax.dynamic_index_in_dim(vb, blk, axis=2, keepdims=False)
        k_g = gather_blocks(kb, selc)
        v_g = gather_blocks(vb, selc)
        s_sel = jnp.einsum('bhtd,bhtkld->bhtkl', qc, k_g,
                           preferred_element_type=jnp.float32) * scale
        s_sel = jnp.where(validc[..., None], s_sel, NEG)
        s_own = jnp.einsum('bhtd,bhld->bhtl', qc, k_own,
                           preferred_element_type=jnp.float32) * scale
        t_pos = t0 + jnp.arange(Q_CHUNK)
        k_pos = blk * MOBA_BLOCK + jnp.arange(MOBA_BLOCK)
        s_own = jnp.where(k_pos[None, :] <= t_pos[:, None], s_own, NEG)
        logits = jnp.concatenate(
            [s_sel.reshape(B, H, Q_CHUNK, k_sel * MOBA_BLOCK), s_own], axis=-1)
        p = jax.nn.softmax(logits, axis=-1)
        p_sel = p[..., :k_sel * MOBA_BLOCK].reshape(B, H, Q_CHUNK, k_sel, MOBA_BLOCK).astype(v.dtype)
        p_own = p[..., k_sel * MOBA_BLOCK:].astype(v.dtype)
        return (jnp.einsum('bhtkl,bhtkld->bhtd', p_sel, v_g)
                + jnp.einsum('bhtl,bhld->bhtd', p_own, v_own))

    out = lax.map(chunk_fn, jnp.arange(S_pad // Q_CHUNK))
    out = out.transpose(1, 2, 0, 3, 4).reshape(B, H, S_pad, Dh)
    return out[:, :, :S]


def spatial_gating(u, vg, w_s, b_s, g_norm):
    B, S, _ = u.shape
    u = jax.nn.gelu(u)
    vg = jax.nn.gelu(vg).reshape(B, S // SGU_CHUNK, SGU_CHUNK, N_SGU_GROUPS, SGU_GROUP_DIM)
    vg = layer_norm(vg, g_norm.reshape(N_SGU_GROUPS, SGU_GROUP_DIM))
    causal = jnp.tril(jnp.ones((SGU_CHUNK, SGU_CHUNK), dtype=bool))
    w = jnp.where(causal[None], w_s, jnp.zeros((), w_s.dtype))
    mixed = jnp.einsum('gts,bnsgc->bntgc', w, vg) + b_s.T[None, None, :, :, None]
    return u * mixed.reshape(B, S, SGU_WIDTH)


def setup_inputs(seed: int = 0) -> dict:
    key = jax.random.key(seed)
    ks = jax.random.split(key, 17)
    f32 = jnp.float32

    def nrm(k, shape, scale):
        return jax.random.normal(k, shape, f32) * scale

    def gain(k, shape):
        return 1.0 + 0.02 * jax.random.normal(k, shape, f32)

    return {
        "x": nrm(ks[0], (BATCH, SEQ, D_MODEL), 1.0),
        "c": nrm(ks[1], (BATCH, D_MODEL), 1.0),
        "w_ada": nrm(ks[2], (DEPTH, D_MODEL, N_MOD * D_MODEL), D_MODEL ** -0.5),
        "b_ada": nrm(ks[3], (DEPTH, N_MOD * D_MODEL), 0.02),
        "g_pre_mix": gain(ks[4], (DEPTH, D_MODEL)),
        "g_post_mix": gain(ks[5], (DEPTH, D_MODEL)),
        "w_in": nrm(ks[6], (DEPTH, D_MODEL, IN_WIDTH), D_MODEL ** -0.5),
        "g_sgu_norm": gain(ks[7], (DEPTH, SGU_WIDTH)),
        "w_sgu": nrm(ks[8], (DEPTH, N_SGU_GROUPS, SGU_CHUNK, SGU_CHUNK), SGU_CHUNK ** -0.5),
        "b_sgu": gain(ks[9], (DEPTH, N_SGU_GROUPS, SGU_CHUNK)),
        "g_attn_out": gain(ks[10], (DEPTH, ATTN_WIDTH)),
        "g_sgu_out": gain(ks[11], (DEPTH, SGU_WIDTH)),
        "w_out": nrm(ks[12], (DEPTH, D_MODEL, D_MODEL), D_MODEL ** -0.5),
        "g_pre_ffn": gain(ks[13], (DEPTH, D_MODEL)),
        "g_post_ffn": gain(ks[14], (DEPTH, D_MODEL)),
        "w_gate_up": nrm(ks[15], (DEPTH, D_MODEL, 2 * D_FF), D_MODEL ** -0.5),
        "w_down": nrm(ks[16], (DEPTH, D_FF, D_MODEL), D_FF ** -0.5),
    }


def reference(x, c, w_ada, b_ada, g_pre_mix, g_post_mix, w_in, g_sgu_norm, w_sgu, b_sgu,
              g_attn_out, g_sgu_out, w_out, g_pre_ffn, g_post_ffn, w_gate_up, w_down):
    B, S, D = x.shape
    c_act = jax.nn.silu(c)
    for l in range(DEPTH):
        mod = c_act @ w_ada[l] + b_ada[l]
        shift_m, scale_m, gate_m, shift_f, scale_f, gate_f = jnp.split(mod, N_MOD, axis=-1)

        h = modulate(rms_norm(x, g_pre_mix[l]), shift_m, scale_m)
        proj = h @ w_in[l]
        q, k, v, u, vg = jnp.split(
            proj, [ATTN_WIDTH, 2 * ATTN_WIDTH, 3 * ATTN_WIDTH, 3 * ATTN_WIDTH + SGU_WIDTH], axis=-1)
        to_heads = lambda t: t.reshape(B, S, N_ATTN_HEADS, HEAD_DIM).transpose(0, 2, 1, 3)
        attn = moba_attention(to_heads(q), to_heads(k), to_heads(v))
        attn = attn.transpose(0, 2, 1, 3).reshape(B, S, ATTN_WIDTH)
        sgu = spatial_gating(u, vg, w_sgu[l], b_sgu[l], g_sgu_norm[l])
        mix = jnp.concatenate([rms_norm(attn, g_attn_out[l]), rms_norm(sgu, g_sgu_out[l])], axis=-1)
        mix = mix @ w_out[l]
        x = x + gate_m[:, None, :] * rms_norm(mix, g_post_mix[l])

        h = modulate(rms_norm(x, g_pre_ffn[l]), shift_f, scale_f)
        g, up = jnp.split(h @ w_gate_up[l], 2, axis=-1)
        y = (jax.nn.silu(g) * up) @ w_down[l]
        x = x + gate_f[:, None, :] * rms_norm(y, g_post_ffn[l])
    return x
```

```python
import functools
import math

import jax
import jax.numpy as jnp
from jax import lax
from jax.experimental import pallas as pl
from jax.experimental.pallas import tpu as pltpu

N_HEADS = 8
HEAD_DIM = 64
ATTN_WIDTH = N_HEADS * HEAD_DIM
N_SGU_GROUPS = 4
SGU_CHUNK = 128
MOBA_BLOCK = 256
MOBA_TOPK = 3
N_MOD = 6
EPS = 1e-6
NEG = -1e30

LANES = 128
VMEM_LIMIT_BYTES = 56 * 1024 * 1024
Q_SCALE = HEAD_DIM ** -0.5 * math.log2(math.e)

BF16 = jnp.bfloat16
F32 = jnp.float32


def _rms(x, g):
    return x * lax.rsqrt(jnp.mean(x * x, axis=-1, keepdims=True) + EPS) * g


def _dot(a, b):
    return jnp.dot(a, b, preferred_element_type=F32)


def _dot_nt(a, b):
    return lax.dot_general(a, b, (((1,), (1,)), ((), ())), preferred_element_type=F32)


def _ada_kernel(c_ref, w_ref, b_ref, o_ref):
    c = c_ref[...]
    c_act = (c * jax.nn.sigmoid(c)).astype(BF16)
    o_ref[...] = _dot(c_act, w_ref[...].astype(BF16)) + b_ref[...]


def _ada(c, w_ada, b_ada):
    bsz, d = c.shape
    n = w_ada.shape[1]
    tn = d
    return pl.pallas_call(
        _ada_kernel,
        grid=(n // tn,),
        in_specs=[
            pl.BlockSpec((bsz, d), lambda j: (0, 0)),
            pl.BlockSpec((d, tn), lambda j: (0, j)),
            pl.BlockSpec((1, tn), lambda j: (0, j)),
        ],
        out_specs=pl.BlockSpec((bsz, tn), lambda j: (0, j)),
        out_shape=jax.ShapeDtypeStruct((bsz, n), F32),
        compiler_params=pltpu.CompilerParams(
            dimension_semantics=("arbitrary",), vmem_limit_bytes=VMEM_LIMIT_BYTES),
        name="ada",
    )(c, w_ada, b_ada.reshape(1, n))


def _in_proj_kernel(x_ref, mod_ref, g_pre_ref, w_in_ref, g_ln_ref, w_sgu_ref, b_sgu_ref,
                    g_sgu_out_ref, q_ref, k_ref, v_ref, sgu_ref, sgu_scr):
    tm = x_ref.shape[1]
    aw = ATTN_WIDTH
    gd = LANES
    n_chunks = tm // SGU_CHUNK

    x = x_ref[0]
    shift = mod_ref[0, 0:1, :]
    scale = mod_ref[0, 1:2, :]
    h = _rms(x, g_pre_ref[...]) * (1.0 + scale) + shift
    proj = _dot(h.astype(BF16), w_in_ref[...])

    q_ref[0] = (proj[:, 0:aw] * Q_SCALE).astype(BF16)
    k_ref[0] = proj[:, aw:2 * aw].astype(BF16)
    v_ref[0] = proj[:, 2 * aw:3 * aw].astype(BF16)

    u = jax.nn.gelu(proj[:, 3 * aw:3 * aw + N_SGU_GROUPS * gd])
    vg = jax.nn.gelu(proj[:, 3 * aw + N_SGU_GROUPS * gd:])

    row = lax.broadcasted_iota(jnp.int32, (SGU_CHUNK, SGU_CHUNK), 0)
    col = lax.broadcasted_iota(jnp.int32, (SGU_CHUNK, SGU_CHUNK), 1)
    causal = col <= row
    for g in range(N_SGU_GROUPS):
        lo = g * gd
        vgg = vg[:, lo:lo + gd]
        d = vgg - jnp.mean(vgg, axis=-1, keepdims=True)
        yn = d * lax.rsqrt(jnp.mean(d * d, axis=-1, keepdims=True) + EPS) * g_ln_ref[:, lo:lo + gd]
        yn = yn.astype(BF16)
        rhs = jnp.concatenate(
            [yn[n * SGU_CHUNK:(n + 1) * SGU_CHUNK, :] for n in range(n_chunks)], axis=1)
        w = jnp.where(causal, w_sgu_ref[g], 0.0).astype(BF16)
        mixed = _dot(w, rhs) + b_sgu_ref[:, g:g + 1]
        for n in range(n_chunks):
            r0 = n * SGU_CHUNK
            sgu_scr[r0:r0 + SGU_CHUNK, lo:lo + gd] = (
                u[r0:r0 + SGU_CHUNK, lo:lo + gd] * mixed[:, n * gd:(n + 1) * gd])
    sgu_ref[0] = _rms(sgu_scr[...], g_sgu_out_ref[...]).astype(BF16)


def _in_proj(x, mod, g_pre, w_in, g_ln, w_sgu, b_sgu_t, g_sgu_out, tm):
    bsz, s, d = x.shape
    in_w = w_in.shape[1]
    aw = ATTN_WIDTH
    sw = in_w - 3 * aw
    sgu_w = sw // 2
    row_spec = lambda w: pl.BlockSpec((1, tm, w), lambda b, t: (b, t, 0))
    full = lambda shape: pl.BlockSpec(shape, lambda b, t: (0,) * len(shape))
    out_sds = jax.ShapeDtypeStruct((bsz, s, aw), BF16)
    return pl.pallas_call(
        _in_proj_kernel,
        grid=(bsz, s // tm),
        in_specs=[
            row_spec(d),
            pl.BlockSpec((1, N_MOD, d), lambda b, t: (b, 0, 0)),
            full((1, d)),
            full((d, in_w)),
            full((1, sgu_w)),
            full((N_SGU_GROUPS, SGU_CHUNK, SGU_CHUNK)),
            full((SGU_CHUNK, N_SGU_GROUPS)),
            full((1, sgu_w)),
        ],
        out_specs=[row_spec(aw), row_spec(aw), row_spec(aw), row_spec(sgu_w)],
        out_shape=[out_sds, out_sds, out_sds, jax.ShapeDtypeStruct((bsz, s, sgu_w), BF16)],
        scratch_shapes=[pltpu.VMEM((tm, sgu_w), F32)],
        compiler_params=pltpu.CompilerParams(
            dimension_semantics=("parallel", "parallel"), vmem_limit_bytes=VMEM_LIMIT_BYTES),
        name="in_proj",
    )(x, mod, g_pre, w_in, g_ln, w_sgu, b_sgu_t, g_sgu_out)


def _topk_bias(gate, n_past, base, lane):
    rel = lane - base
    valid = jnp.logical_and(rel >= 0, rel < n_past)
    neg_inf = jnp.float32(-jnp.inf)
    lane_f = lane.astype(F32)
    g = jnp.where(valid, gate, neg_inf)
    keep = jnp.zeros_like(gate)
    for _ in range(MOBA_TOPK):
        mx = jnp.max(g, axis=-1, keepdims=True)
        cand = jnp.where(g == mx, jnp.where(g > neg_inf, lane_f, 1e9), 1e9)
        first = jnp.min(cand, axis=-1, keepdims=True)
        pick = lane_f == first
        keep = jnp.where(pick, 1.0, keep)
        g = jnp.where(pick, neg_inf, g)
    return jnp.where(valid, jnp.where(keep > 0.0, 0.0, NEG), 0.0)


def _moba_kernel(q_ref, k_ref, v_ref, o_ref, k_ext, v_ext, s_scr, m_scr, acc_scr):
    seq = k_ref.shape[1]
    blk = MOBA_BLOCK
    nb = seq // blk
    half = HEAD_DIM
    lane = lax.broadcasted_iota(jnp.int32, (blk, LANES), 1)
    in_head = (lane < half, lane >= half)
    bias_base = (half, 0)

    for j in range(nb):
        kb = k_ref[0, j * blk:(j + 1) * blk, :].astype(F32)
        vb = v_ref[0, j * blk:(j + 1) * blk, :].astype(F32)
        for hd in range(2):
            onehot = jnp.where(lane == bias_base[hd] + j, 1.0, 0.0)
            k_ext[hd, j] = jnp.where(in_head[hd], kb, onehot).astype(BF16)
            v_ext[hd, j] = jnp.where(in_head[hd], vb, 1.0).astype(BF16)

    prow = lax.broadcasted_iota(jnp.int32, (LANES, seq), 0)
    pcol = lax.broadcasted_iota(jnp.int32, (LANES, seq), 1)
    pr = jnp.bitwise_and(prow, half - 1)
    pblk = lax.shift_right_logical(pcol, blk.bit_length() - 1)
    psel = jnp.where(jnp.logical_and(pr == pblk, pr < nb), 1.0 / blk, 0.0).astype(BF16)
    km_full = _dot(psel, k_ref[0])
    krow = lax.broadcasted_iota(jnp.int32, (LANES, LANES), 0)
    klane = lax.broadcasted_iota(jnp.int32, (LANES, LANES), 1)
    km = (
        jnp.where(jnp.logical_and(krow >= half, klane < half), km_full, 0.0).astype(BF16),
        jnp.where(jnp.logical_and(krow < half, klane >= half), km_full, 0.0).astype(BF16),
    )

    row = lax.broadcasted_iota(jnp.int32, (blk, blk), 0)
    col = lax.broadcasted_iota(jnp.int32, (blk, blk), 1)
    causal = col <= row

    def q_tile(i, carry):
        r0 = pl.multiple_of(i * blk, blk)
        qp = q_ref[0, pl.ds(r0, blk), :].astype(F32)
        outs = []
        for hd in range(2):
            q0 = jnp.where(in_head[hd], qp, 0.0).astype(BF16)
            gate = _dot_nt(q0, km[hd])
            bias = _topk_bias(gate, i, bias_base[hd], lane)
            q_ext = jnp.where(in_head[hd], qp, bias).astype(BF16)

            s_own = jnp.where(causal, _dot_nt(q_ext, k_ext[hd, i]), NEG)
            s_scr[i] = s_own
            m_scr[...] = jnp.maximum(s_own[:, :LANES], s_own[:, LANES:])

            def pass1(j, c):
                s = _dot_nt(q_ext, k_ext[hd, j])
                s_scr[j] = s
                m_scr[...] = jnp.maximum(m_scr[...], jnp.maximum(s[:, :LANES], s[:, LANES:]))
                return c

            lax.fori_loop(0, i, pass1, 0)
            m = jnp.max(m_scr[...], axis=-1, keepdims=True)

            def pass2(j, c):
                p = jnp.exp2(s_scr[j] - m).astype(BF16)
                acc_scr[...] += _dot(p, v_ext[hd, j])
                return c

            acc_scr[...] = jnp.zeros_like(acc_scr)
            lax.fori_loop(0, i + 1, pass2, 0)
            outs.append(acc_scr[...])
        num = jnp.where(in_head[0], outs[0], outs[1])
        den = pltpu.roll(jnp.where(in_head[0], outs[1], outs[0]), half, axis=1)
        o_ref[0, pl.ds(r0, blk), :] = (num / den).astype(o_ref.dtype)
        return carry

    lax.fori_loop(0, nb, q_tile, 0)


def _moba(q, k, v):
    bsz, s, aw = q.shape
    nb = s // MOBA_BLOCK
    spec = pl.BlockSpec((1, s, LANES), lambda b, p: (b, 0, p))
    return pl.pallas_call(
        _moba_kernel,
        grid=(bsz, aw // LANES),
        in_specs=[spec, spec, spec],
        out_specs=spec,
        out_shape=jax.ShapeDtypeStruct((bsz, s, aw), BF16),
        scratch_shapes=[
            pltpu.VMEM((2, nb, MOBA_BLOCK, LANES), BF16),
            pltpu.VMEM((2, nb, MOBA_BLOCK, LANES), BF16),
            pltpu.VMEM((nb, MOBA_BLOCK, MOBA_BLOCK), F32),
            pltpu.VMEM((MOBA_BLOCK, LANES), F32),
            pltpu.VMEM((MOBA_BLOCK, LANES), F32),
        ],
        compiler_params=pltpu.CompilerParams(
            dimension_semantics=("parallel", "parallel"), vmem_limit_bytes=VMEM_LIMIT_BYTES),
        name="moba",
    )(q, k, v)


def _mix_out_kernel(attn_ref, sgu_ref, x_ref, mod_ref, g_attn_ref, w_out_ref, g_post_ref, o_ref):
    aw = attn_ref.shape[2]
    a = _rms(attn_ref[0].astype(F32), g_attn_ref[...]).astype(BF16)
    mix = _dot(a, w_out_ref[0:aw, :]) + _dot(sgu_ref[0], w_out_ref[aw:, :])
    gate = mod_ref[0, 2:3, :]
    o_ref[0] = x_ref[0] + gate * _rms(mix, g_post_ref[...])


def _mix_out(attn, sgu, x, mod, g_attn, w_out, g_post, tm):
    bsz, s, d = x.shape
    aw = attn.shape[2]
    row_spec = lambda w: pl.BlockSpec((1, tm, w), lambda b, t: (b, t, 0))
    full = lambda shape: pl.BlockSpec(shape, lambda b, t: (0,) * len(shape))
    return pl.pallas_call(
        _mix_out_kernel,
        grid=(bsz, s // tm),
        in_specs=[
            row_spec(aw), row_spec(sgu.shape[2]), row_spec(d),
            pl.BlockSpec((1, N_MOD, d), lambda b, t: (b, 0, 0)),
            full((1, aw)), full(w_out.shape), full((1, d)),
        ],
        out_specs=row_spec(d),
        out_shape=jax.ShapeDtypeStruct((bsz, s, d), F32),
        compiler_params=pltpu.CompilerParams(
            dimension_semantics=("parallel", "parallel"), vmem_limit_bytes=VMEM_LIMIT_BYTES),
        name="mix_out",
    )(attn, sgu, x, mod, g_attn, w_out, g_post)


def _ffn_kernel(x_ref, mod_ref, g_pre_ref, w_gu_ref, w_down_ref, g_post_ref, o_ref, act_scr, *,
                chunk):
    d_ff = w_down_ref.shape[0]
    x = x_ref[0]
    shift = mod_ref[0, 3:4, :]
    scale = mod_ref[0, 4:5, :]
    gate = mod_ref[0, 5:6, :]
    h = (_rms(x, g_pre_ref[...]) * (1.0 + scale) + shift).astype(BF16)
    for c0 in range(0, d_ff, chunk):
        g = _dot(h, w_gu_ref[:, c0:c0 + chunk])
        up = _dot(h, w_gu_ref[:, d_ff + c0:d_ff + c0 + chunk])
        act_scr[:, c0:c0 + chunk] = (g * jax.nn.sigmoid(g) * up).astype(BF16)
    y = _dot(act_scr[...], w_down_ref[...])
    o_ref[0] = x + gate * _rms(y, g_post_ref[...])


def _ffn(x, mod, g_pre, w_gu, w_down, g_post, tm, chunk):
    bsz, s, d = x.shape
    d_ff = w_down.shape[0]
    row_spec = pl.BlockSpec((1, tm, d), lambda b, t: (b, t, 0))
    full = lambda shape: pl.BlockSpec(shape, lambda b, t: (0,) * len(shape))
    return pl.pallas_call(
        functools.partial(_ffn_kernel, chunk=chunk),
        grid=(bsz, s // tm),
        in_specs=[
            row_spec,
            pl.BlockSpec((1, N_MOD, d), lambda b, t: (b, 0, 0)),
            full((1, d)), full(w_gu.shape), full(w_down.shape), full((1, d)),
        ],
        out_specs=row_spec,
        out_shape=jax.ShapeDtypeStruct((bsz, s, d), F32),
        scratch_shapes=[pltpu.VMEM((tm, d_ff), BF16)],
        compiler_params=pltpu.CompilerParams(
            dimension_semantics=("parallel", "parallel"), vmem_limit_bytes=VMEM_LIMIT_BYTES),
        name="ffn",
    )(x, mod, g_pre, w_gu, w_down, g_post)


def kernel(x, c, w_ada, b_ada, g_pre_mix, g_post_mix, w_in, g_sgu_norm, w_sgu, b_sgu,
           g_attn_out, g_sgu_out, w_out, g_pre_ffn, g_post_ffn, w_gate_up, w_down):
    bsz, s, d = x.shape
    depth = w_ada.shape[0]
    d_ff = w_down.shape[1]
    assert s % MOBA_BLOCK == 0 and s // MOBA_BLOCK <= HEAD_DIM
    assert w_in.shape[2] == 3 * ATTN_WIDTH + 2 * N_SGU_GROUPS * LANES
    tm_in, tm_mix, tm_ffn = 512, 512, 512
    ffn_chunk = 256
    assert d_ff % ffn_chunk == 0
    row = lambda a: a.reshape(1, -1)
    for l in range(depth):
        mod = _ada(c, w_ada[l], b_ada[l]).reshape(bsz, N_MOD, d)
        q, k, v, sgu = _in_proj(
            x, mod, row(g_pre_mix[l]), w_in[l].astype(BF16), row(g_sgu_norm[l]), w_sgu[l],
            b_sgu[l].T, row(g_sgu_out[l]), tm_in)
        attn = _moba(q, k, v)
        x = _mix_out(attn, sgu, x, mod, row(g_attn_out[l]), w_out[l].astype(BF16),
                     row(g_post_mix[l]), tm_mix)
        x = _ffn(x, mod, row(g_pre_ffn[l]), w_gate_up[l].astype(BF16), w_down[l].astype(BF16),
                 row(g_post_ffn[l]), tm_ffn, ffn_chunk)
    return x
```

```python
import functools
import math

import jax
import jax.numpy as jnp
from jax import lax
from jax.experimental import pallas as pl
from jax.experimental.pallas import tpu as pltpu

N_HEADS = 8
HEAD_DIM = 64
ATTN_WIDTH = N_HEADS * HEAD_DIM
N_SGU_GROUPS = 4
SGU_CHUNK = 128
MOBA_BLOCK = 256
MOBA_TOPK = 3
N_MOD = 6
EPS = 1e-6
NEG = -1e30

LANES = 128
VMEM_LIMIT_BYTES = 56 * 1024 * 1024
Q_SCALE = HEAD_DIM ** -0.5 * math.log2(math.e)

BF16 = jnp.bfloat16
F32 = jnp.float32


def _rms(x, g):
    return x * lax.rsqrt(jnp.mean(x * x, axis=-1, keepdims=True) + EPS) * g


def _dot(a, b):
    return jnp.dot(a, b, preferred_element_type=F32)


def _dot_nt(a, b):
    return lax.dot_general(a, b, (((1,), (1,)), ((), ())), preferred_element_type=F32)


def _ada_kernel(c_ref, w_ref, b_ref, o_ref):
    c = c_ref[...]
    c_act = (c * jax.nn.sigmoid(c)).astype(BF16)
    o_ref[...] = _dot(c_act, w_ref[...].astype(BF16)) + b_ref[...]


def _ada(c, w_ada, b_ada):
    bsz, d = c.shape
    n = w_ada.shape[1]
    tn = d
    return pl.pallas_call(
        _ada_kernel,
        grid=(n // tn,),
        in_specs=[
            pl.BlockSpec((bsz, d), lambda j: (0, 0)),
            pl.BlockSpec((d, tn), lambda j: (0, j)),
            pl.BlockSpec((1, tn), lambda j: (0, j)),
        ],
        out_specs=pl.BlockSpec((bsz, tn), lambda j: (0, j)),
        out_shape=jax.ShapeDtypeStruct((bsz, n), F32),
        compiler_params=pltpu.CompilerParams(
            dimension_semantics=("arbitrary",), vmem_limit_bytes=VMEM_LIMIT_BYTES),
        name="ada",
    )(c, w_ada, b_ada.reshape(1, n))


def _in_proj_kernel(x_ref, mod_ref, g_pre_ref, w_in_ref, g_ln_ref, w_sgu_ref, b_sgu_ref,
                    g_sgu_out_ref, q_ref, k_ref, v_ref, sgu_ref, sgu_scr):
    tm = x_ref.shape[1]
    aw = ATTN_WIDTH
    gd = LANES
    n_chunks = tm // SGU_CHUNK

    x = x_ref[0]
    shift = mod_ref[0, 0:1, :]
    scale = mod_ref[0, 1:2, :]
    h = _rms(x, g_pre_ref[...]) * (1.0 + scale) + shift
    proj = _dot(h.astype(BF16), w_in_ref[...])

    q_ref[0] = (proj[:, 0:aw] * Q_SCALE).astype(BF16)
    k_ref[0] = proj[:, aw:2 * aw].astype(BF16)
    v_ref[0] = proj[:, 2 * aw:3 * aw].astype(BF16)

    u = jax.nn.gelu(proj[:, 3 * aw:3 * aw + N_SGU_GROUPS * gd])
    vg = jax.nn.gelu(proj[:, 3 * aw + N_SGU_GROUPS * gd:])

    row = lax.broadcasted_iota(jnp.int32, (SGU_CHUNK, SGU_CHUNK), 0)
    col = lax.broadcasted_iota(jnp.int32, (SGU_CHUNK, SGU_CHUNK), 1)
    causal = col <= row
    for g in range(N_SGU_GROUPS):
        lo = g * gd
        vgg = vg[:, lo:lo + gd]
        d = vgg - jnp.mean(vgg, axis=-1, keepdims=True)
        yn = d * lax.rsqrt(jnp.mean(d * d, axis=-1, keepdims=True) + EPS) * g_ln_ref[:, lo:lo + gd]
        yn = yn.astype(BF16)
        rhs = jnp.concatenate(
            [yn[n * SGU_CHUNK:(n + 1) * SGU_CHUNK, :] for n in range(n_chunks)], axis=1)
        w = jnp.where(causal, w_sgu_ref[g], 0.0).astype(BF16)
        mixed = _dot(w, rhs) + b_sgu_ref[:, g:g + 1]
        for n in range(n_chunks):
            r0 = n * SGU_CHUNK
            sgu_scr[r0:r0 + SGU_CHUNK, lo:lo + gd] = (
                u[r0:r0 + SGU_CHUNK, lo:lo + gd] * mixed[:, n * gd:(n + 1) * gd])
    sgu_ref[0] = _rms(sgu_scr[...], g_sgu_out_ref[...]).astype(BF16)


def _in_proj(x, mod, g_pre, w_in, g_ln, w_sgu, b_sgu_t, g_sgu_out, tm):
    bsz, s, d = x.shape
    in_w = w_in.shape[1]
    aw = ATTN_WIDTH
    sw = in_w - 3 * aw
    sgu_w = sw // 2
    row_spec = lambda w: pl.BlockSpec((1, tm, w), lambda b, t: (b, t, 0))
    full = lambda shape: pl.BlockSpec(shape, lambda b, t: (0,) * len(shape))
    out_sds = jax.ShapeDtypeStruct((bsz, s, aw), BF16)
    return pl.pallas_call(
        _in_proj_kernel,
        grid=(bsz, s // tm),
        in_specs=[
            row_spec(d),
            pl.BlockSpec((1, N_MOD, d), lambda b, t: (b, 0, 0)),
            full((1, d)),
            full((d, in_w)),
            full((1, sgu_w)),
            full((N_SGU_GROUPS, SGU_CHUNK, SGU_CHUNK)),
            full((SGU_CHUNK, N_SGU_GROUPS)),
            full((1, sgu_w)),
        ],
        out_specs=[row_spec(aw), row_spec(aw), row_spec(aw), row_spec(sgu_w)],
        out_shape=[out_sds, out_sds, out_sds, jax.ShapeDtypeStruct((bsz, s, sgu_w), BF16)],
        scratch_shapes=[pltpu.VMEM((tm, sgu_w), F32)],
        compiler_params=pltpu.CompilerParams(
            dimension_semantics=("parallel", "parallel"), vmem_limit_bytes=VMEM_LIMIT_BYTES),
        name="in_proj",
    )(x, mod, g_pre, w_in, g_ln, w_sgu, b_sgu_t, g_sgu_out)


def _topk_bias(gate, n_past, base, lane):
    rel = lane - base
    valid = jnp.logical_and(rel >= 0, rel < n_past)
    neg_inf = jnp.float32(-jnp.inf)
    lane_f = lane.astype(F32)
    g = jnp.where(valid, gate, neg_inf)
    keep = jnp.zeros_like(gate)
    for _ in range(MOBA_TOPK):
        mx = jnp.max(g, axis=-1, keepdims=True)
        cand = jnp.where(g == mx, jnp.where(g > neg_inf, lane_f, 1e9), 1e9)
        first = jnp.min(cand, axis=-1, keepdims=True)
        pick = lane_f == first
        keep = jnp.where(pick, 1.0, keep)
        g = jnp.where(pick, neg_inf, g)
    return jnp.where(valid, jnp.where(keep > 0.0, 0.0, NEG), 0.0)


def _moba_kernel(q_ref, k_ref, v_ref, o_ref, k_ext, v_ext, s_scr, p_scr, *, chunk):
    seq = k_ref.shape[1]
    blk = MOBA_BLOCK
    nb = seq // blk
    half = HEAD_DIM
    lane = lax.broadcasted_iota(jnp.int32, (blk, LANES), 1)
    in_head = (lane < half, lane >= half)
    bias_base = (half, 0)

    for j in range(nb):
        rows = slice(j * blk, (j + 1) * blk)
        kb = k_ref[0, rows, :].astype(F32)
        vb = v_ref[0, rows, :].astype(F32)
        for hd in range(2):
            onehot = jnp.where(lane == bias_base[hd] + j, 1.0, 0.0)
            k_ext[hd, rows, :] = jnp.where(in_head[hd], kb, onehot).astype(BF16)
            v_ext[hd, rows, :] = jnp.where(in_head[hd], vb, 1.0).astype(BF16)

    prow = lax.broadcasted_iota(jnp.int32, (LANES, seq), 0)
    pcol = lax.broadcasted_iota(jnp.int32, (LANES, seq), 1)
    pr = jnp.bitwise_and(prow, half - 1)
    pblk = lax.shift_right_logical(pcol, blk.bit_length() - 1)
    psel = jnp.where(jnp.logical_and(pr == pblk, pr < nb), 1.0 / blk, 0.0).astype(BF16)
    km_full = _dot(psel, k_ref[0])
    krow = lax.broadcasted_iota(jnp.int32, (LANES, LANES), 0)
    klane = lax.broadcasted_iota(jnp.int32, (LANES, LANES), 1)
    km = (
        jnp.where(jnp.logical_and(krow >= half, klane < half), km_full, 0.0).astype(BF16),
        jnp.where(jnp.logical_and(krow < half, klane >= half), km_full, 0.0).astype(BF16),
    )

    row = lax.broadcasted_iota(jnp.int32, (blk, blk), 0)
    col = lax.broadcasted_iota(jnp.int32, (blk, blk), 1)
    causal = col <= row

    for i in range(nb):
        length = (i + 1) * blk
        qp = q_ref[0, i * blk:(i + 1) * blk, :].astype(F32)
        outs = []
        for hd in range(2):
            q0 = jnp.where(in_head[hd], qp, 0.0).astype(BF16)
            gate = _dot_nt(q0, km[hd])
            bias = _topk_bias(gate, i, bias_base[hd], lane)
            q_ext = jnp.where(in_head[hd], qp, bias).astype(BF16)

            mrun = None
            for c0 in range(0, length, chunk):
                cw = min(chunk, length - c0)
                s = _dot_nt(q_ext, k_ext[hd, c0:c0 + cw, :])
                if c0 + cw == length:
                    own = jnp.where(causal, s[:, cw - blk:], NEG)
                    s = own if cw == blk else jnp.concatenate([s[:, :cw - blk], own], axis=1)
                s_scr[hd, :, c0:c0 + cw] = s
                for l0 in range(0, cw, LANES):
                    part = s[:, l0:l0 + LANES]
                    mrun = part if mrun is None else jnp.maximum(mrun, part)
            m = jnp.max(mrun, axis=-1, keepdims=True)
            for c0 in range(0, length, chunk):
                cw = min(chunk, length - c0)
                p_scr[hd, :, c0:c0 + cw] = jnp.exp2(s_scr[hd, :, c0:c0 + cw] - m).astype(BF16)
            outs.append(_dot(p_scr[hd, :, 0:length], v_ext[hd, 0:length, :]))
        num = jnp.where(in_head[0], outs[0], outs[1])
        den = pltpu.roll(jnp.where(in_head[0], outs[1], outs[0]), half, axis=1)
        o_ref[0, i * blk:(i + 1) * blk, :] = (num / den).astype(o_ref.dtype)


def _moba(q, k, v, chunk):
    bsz, s, aw = q.shape
    spec = pl.BlockSpec((1, s, LANES), lambda b, p: (b, 0, p))
    return pl.pallas_call(
        functools.partial(_moba_kernel, chunk=chunk),
        grid=(bsz, aw // LANES),
        in_specs=[spec, spec, spec],
        out_specs=spec,
        out_shape=jax.ShapeDtypeStruct((bsz, s, aw), BF16),
        scratch_shapes=[
            pltpu.VMEM((2, s, LANES), BF16),
            pltpu.VMEM((2, s, LANES), BF16),
            pltpu.VMEM((2, MOBA_BLOCK, s), F32),
            pltpu.VMEM((2, MOBA_BLOCK, s), BF16),
        ],
        compiler_params=pltpu.CompilerParams(
            dimension_semantics=("parallel", "parallel"), vmem_limit_bytes=VMEM_LIMIT_BYTES),
        name="moba",
    )(q, k, v)


def _mix_out_kernel(attn_ref, sgu_ref, x_ref, mod_ref, g_attn_ref, w_out_ref, g_post_ref, o_ref):
    aw = attn_ref.shape[2]
    a = _rms(attn_ref[0].astype(F32), g_attn_ref[...]).astype(BF16)
    mix = _dot(a, w_out_ref[0:aw, :]) + _dot(sgu_ref[0], w_out_ref[aw:, :])
    gate = mod_ref[0, 2:3, :]
    o_ref[0] = x_ref[0] + gate * _rms(mix, g_post_ref[...])


def _mix_out(attn, sgu, x, mod, g_attn, w_out, g_post, tm):
    bsz, s, d = x.shape
    aw = attn.shape[2]
    row_spec = lambda w: pl.BlockSpec((1, tm, w), lambda b, t: (b, t, 0))
    full = lambda shape: pl.BlockSpec(shape, lambda b, t: (0,) * len(shape))
    return pl.pallas_call(
        _mix_out_kernel,
        grid=(bsz, s // tm),
        in_specs=[
            row_spec(aw), row_spec(sgu.shape[2]), row_spec(d),
            pl.BlockSpec((1, N_MOD, d), lambda b, t: (b, 0, 0)),
            full((1, aw)), full(w_out.shape), full((1, d)),
        ],
        out_specs=row_spec(d),
        out_shape=jax.ShapeDtypeStruct((bsz, s, d), F32),
        compiler_params=pltpu.CompilerParams(
            dimension_semantics=("parallel", "parallel"), vmem_limit_bytes=VMEM_LIMIT_BYTES),
        name="mix_out",
    )(attn, sgu, x, mod, g_attn, w_out, g_post)


def _ffn_kernel(x_ref, mod_ref, g_pre_ref, w_gu_ref, w_down_ref, g_post_ref, o_ref, act_scr, *,
                chunk):
    d_ff = w_down_ref.shape[0]
    x = x_ref[0]
    shift = mod_ref[0, 3:4, :]
    scale = mod_ref[0, 4:5, :]
    gate = mod_ref[0, 5:6, :]
    h = (_rms(x, g_pre_ref[...]) * (1.0 + scale) + shift).astype(BF16)
    for c0 in range(0, d_ff, chunk):
        g = _dot(h, w_gu_ref[:, c0:c0 + chunk])
        up = _dot(h, w_gu_ref[:, d_ff + c0:d_ff + c0 + chunk])
        act_scr[:, c0:c0 + chunk] = (g * jax.nn.sigmoid(g) * up).astype(BF16)
    y = _dot(act_scr[...], w_down_ref[...])
    o_ref[0] = x + gate * _rms(y, g_post_ref[...])


def _ffn(x, mod, g_pre, w_gu, w_down, g_post, tm, chunk):
    bsz, s, d = x.shape
    d_ff = w_down.shape[0]
    row_spec = pl.BlockSpec((1, tm, d), lambda b, t: (b, t, 0))
    full = lambda shape: pl.BlockSpec(shape, lambda b, t: (0,) * len(shape))
    return pl.pallas_call(
        functools.partial(_ffn_kernel, chunk=chunk),
        grid=(bsz, s // tm),
        in_specs=[
            row_spec,
            pl.BlockSpec((1, N_MOD, d), lambda b, t: (b, 0, 0)),
            full((1, d)), full(w_gu.shape), full(w_down.shape), full((1, d)),
        ],
        out_specs=row_spec,
        out_shape=jax.ShapeDtypeStruct((bsz, s, d), F32),
        scratch_shapes=[pltpu.VMEM((tm, d_ff), BF16)],
        compiler_params=pltpu.CompilerParams(
            dimension_semantics=("parallel", "parallel"), vmem_limit_bytes=VMEM_LIMIT_BYTES),
        name="ffn",
    )(x, mod, g_pre, w_gu, w_down, g_post)


def kernel(x, c, w_ada, b_ada, g_pre_mix, g_post_mix, w_in, g_sgu_norm, w_sgu, b_sgu,
           g_attn_out, g_sgu_out, w_out, g_pre_ffn, g_post_ffn, w_gate_up, w_down):
    bsz, s, d = x.shape
    depth = w_ada.shape[0]
    d_ff = w_down.shape[1]
    assert s % MOBA_BLOCK == 0 and s // MOBA_BLOCK <= HEAD_DIM
    assert w_in.shape[2] == 3 * ATTN_WIDTH + 2 * N_SGU_GROUPS * LANES
    tm_in, tm_mix, tm_ffn = 512, 512, 512
    ffn_chunk = 256
    moba_chunk = 2 * MOBA_BLOCK
    assert d_ff % ffn_chunk == 0
    row = lambda a: a.reshape(1, -1)
    for l in range(depth):
        mod = _ada(c, w_ada[l], b_ada[l]).reshape(bsz, N_MOD, d)
        q, k, v, sgu = _in_proj(
            x, mod, row(g_pre_mix[l]), w_in[l].astype(BF16), row(g_sgu_norm[l]), w_sgu[l],
            b_sgu[l].T, row(g_sgu_out[l]), tm_in)
        attn = _moba(q, k, v, moba_chunk)
        x = _mix_out(attn, sgu, x, mod, row(g_attn_out[l]), w_out[l].astype(BF16),
                     row(g_post_mix[l]), tm_mix)
        x = _ffn(x, mod, row(g_pre_ffn[l]), w_gate_up[l].astype(BF16), w_down[l].astype(BF16),
                 row(g_post_ffn[l]), tm_ffn, ffn_chunk)
    return x
```

```python
import functools
import math

import jax
import jax.numpy as jnp
from jax import lax
from jax.experimental import pallas as pl
from jax.experimental.pallas import tpu as pltpu

N_HEADS = 8
HEAD_DIM = 64
ATTN_WIDTH = N_HEADS * HEAD_DIM
N_SGU_GROUPS = 4
SGU_CHUNK = 128
MOBA_BLOCK = 256
MOBA_TOPK = 3
N_MOD = 6
EPS = 1e-6
NEG = -1e30

LANES = 128
VMEM_LIMIT_BYTES = 56 * 1024 * 1024
Q_SCALE = HEAD_DIM ** -0.5 * math.log2(math.e)

BF16 = jnp.bfloat16
F32 = jnp.float32


def _rms(x, g):
    return x * lax.rsqrt(jnp.mean(x * x, axis=-1, keepdims=True) + EPS) * g


def _dot(a, b):
    return jnp.dot(a, b, preferred_element_type=F32)


def _dot_nt(a, b):
    return lax.dot_general(a, b, (((1,), (1,)), ((), ())), preferred_element_type=F32)


def _ada_kernel(c_ref, w_ref, b_ref, o_ref):
    c = c_ref[...]
    c_act = (c * jax.nn.sigmoid(c)).astype(BF16)
    o_ref[...] = _dot(c_act, w_ref[...].astype(BF16)) + b_ref[...]


def _ada(c, w_ada, b_ada):
    bsz, d = c.shape
    n = w_ada.shape[1]
    tn = d
    return pl.pallas_call(
        _ada_kernel,
        grid=(n // tn,),
        in_specs=[
            pl.BlockSpec((bsz, d), lambda j: (0, 0)),
            pl.BlockSpec((d, tn), lambda j: (0, j)),
            pl.BlockSpec((1, tn), lambda j: (0, j)),
        ],
        out_specs=pl.BlockSpec((bsz, tn), lambda j: (0, j)),
        out_shape=jax.ShapeDtypeStruct((bsz, n), F32),
        compiler_params=pltpu.CompilerParams(
            dimension_semantics=("arbitrary",), vmem_limit_bytes=VMEM_LIMIT_BYTES),
        name="ada",
    )(c, w_ada, b_ada.reshape(1, n))


def _in_proj_kernel(x_ref, mod_ref, g_pre_ref, w_in_ref, g_ln_ref, w_sgu_ref, b_sgu_ref,
                    g_sgu_out_ref, q_ref, k_ref, v_ref, sgu_ref, sgu_scr):
    tm = x_ref.shape[1]
    aw = ATTN_WIDTH
    gd = LANES
    n_chunks = tm // SGU_CHUNK

    x = x_ref[0]
    shift = mod_ref[0, 0:1, :]
    scale = mod_ref[0, 1:2, :]
    h = _rms(x, g_pre_ref[...]) * (1.0 + scale) + shift
    proj = _dot(h.astype(BF16), w_in_ref[...])

    q_ref[0] = (proj[:, 0:aw] * Q_SCALE).astype(BF16)
    k_ref[0] = proj[:, aw:2 * aw].astype(BF16)
    v_ref[0] = proj[:, 2 * aw:3 * aw].astype(BF16)

    u = jax.nn.gelu(proj[:, 3 * aw:3 * aw + N_SGU_GROUPS * gd])
    vg = jax.nn.gelu(proj[:, 3 * aw + N_SGU_GROUPS * gd:])

    row = lax.broadcasted_iota(jnp.int32, (SGU_CHUNK, SGU_CHUNK), 0)
    col = lax.broadcasted_iota(jnp.int32, (SGU_CHUNK, SGU_CHUNK), 1)
    causal = col <= row
    for g in range(N_SGU_GROUPS):
        lo = g * gd
        vgg = vg[:, lo:lo + gd]
        d = vgg - jnp.mean(vgg, axis=-1, keepdims=True)
        yn = d * lax.rsqrt(jnp.mean(d * d, axis=-1, keepdims=True) + EPS) * g_ln_ref[:, lo:lo + gd]
        yn = yn.astype(BF16)
        rhs = jnp.concatenate(
            [yn[n * SGU_CHUNK:(n + 1) * SGU_CHUNK, :] for n in range(n_chunks)], axis=1)
        w = jnp.where(causal, w_sgu_ref[g], 0.0).astype(BF16)
        mixed = _dot(w, rhs) + b_sgu_ref[:, g:g + 1]
        for n in range(n_chunks):
            r0 = n * SGU_CHUNK
            sgu_scr[r0:r0 + SGU_CHUNK, lo:lo + gd] = (
                u[r0:r0 + SGU_CHUNK, lo:lo + gd] * mixed[:, n * gd:(n + 1) * gd])
    sgu_ref[0] = _rms(sgu_scr[...], g_sgu_out_ref[...]).astype(BF16)


def _in_proj(x, mod, g_pre, w_in, g_ln, w_sgu, b_sgu_t, g_sgu_out, tm):
    bsz, s, d = x.shape
    in_w = w_in.shape[1]
    aw = ATTN_WIDTH
    sw = in_w - 3 * aw
    sgu_w = sw // 2
    row_spec = lambda w: pl.BlockSpec((1, tm, w), lambda b, t: (b, t, 0))
    full = lambda shape: pl.BlockSpec(shape, lambda b, t: (0,) * len(shape))
    out_sds = jax.ShapeDtypeStruct((bsz, s, aw), BF16)
    return pl.pallas_call(
        _in_proj_kernel,
        grid=(bsz, s // tm),
        in_specs=[
            row_spec(d),
            pl.BlockSpec((1, N_MOD, d), lambda b, t: (b, 0, 0)),
            full((1, d)),
            full((d, in_w)),
            full((1, sgu_w)),
            full((N_SGU_GROUPS, SGU_CHUNK, SGU_CHUNK)),
            full((SGU_CHUNK, N_SGU_GROUPS)),
            full((1, sgu_w)),
        ],
        out_specs=[row_spec(aw), row_spec(aw), row_spec(aw), row_spec(sgu_w)],
        out_shape=[out_sds, out_sds, out_sds, jax.ShapeDtypeStruct((bsz, s, sgu_w), BF16)],
        scratch_shapes=[pltpu.VMEM((tm, sgu_w), F32)],
        compiler_params=pltpu.CompilerParams(
            dimension_semantics=("parallel", "parallel"), vmem_limit_bytes=VMEM_LIMIT_BYTES),
        name="in_proj",
    )(x, mod, g_pre, w_in, g_ln, w_sgu, b_sgu_t, g_sgu_out)


def _topk_bias(gate, n_past, base):
    t = gate.shape[0]
    rows = -(-n_past // 8) * 8
    g = gate.T[base:base + rows, :]
    nrow = lax.broadcasted_iota(jnp.int32, (rows, t), 0)
    beaten = jnp.zeros((rows, t), F32)
    for other in range(n_past):
        b = g[other:other + 1, :]
        beaten = beaten + jnp.where(b > g, 1.0, 0.0)
        beaten = beaten + jnp.where(jnp.logical_and(b == g, nrow > other), 1.0, 0.0)
    bias_t = jnp.where(nrow < n_past, jnp.where(beaten < MOBA_TOPK, 0.0, NEG), 0.0)
    pieces = [bias_t]
    if base:
        pieces.insert(0, jnp.zeros((base, t), F32))
    if base + rows < LANES:
        pieces.append(jnp.zeros((LANES - base - rows, t), F32))
    return jnp.concatenate(pieces, axis=0).T


def _moba_kernel(q_ref, k_ref, v_ref, o_ref, q_ext, k_ext, v_ext, s_scr, p_scr, *, chunk):
    seq = k_ref.shape[1]
    blk = MOBA_BLOCK
    nb = seq // blk
    half = HEAD_DIM
    lane = lax.broadcasted_iota(jnp.int32, (blk, LANES), 1)
    in_head = (lane < half, lane >= half)
    bias_base = (half, 0)

    for j in range(nb):
        rows = slice(j * blk, (j + 1) * blk)
        kb = k_ref[0, rows, :].astype(F32)
        vb = v_ref[0, rows, :].astype(F32)
        for hd in range(2):
            onehot = jnp.where(lane == bias_base[hd] + j, 1.0, 0.0)
            k_ext[hd, rows, :] = jnp.where(in_head[hd], kb, onehot).astype(BF16)
            v_ext[hd, rows, :] = jnp.where(in_head[hd], vb, 1.0).astype(BF16)

    prow = lax.broadcasted_iota(jnp.int32, (LANES, seq), 0)
    pcol = lax.broadcasted_iota(jnp.int32, (LANES, seq), 1)
    pr = jnp.bitwise_and(prow, half - 1)
    pblk = lax.shift_right_logical(pcol, blk.bit_length() - 1)
    psel = jnp.where(jnp.logical_and(pr == pblk, pr < nb), 1.0 / blk, 0.0).astype(BF16)
    km_full = _dot(psel, k_ref[0])
    krow = lax.broadcasted_iota(jnp.int32, (LANES, LANES), 0)
    klane = lax.broadcasted_iota(jnp.int32, (LANES, LANES), 1)
    km = (
        jnp.where(jnp.logical_and(krow >= half, klane < half), km_full, 0.0).astype(BF16),
        jnp.where(jnp.logical_and(krow < half, klane >= half), km_full, 0.0).astype(BF16),
    )

    row = lax.broadcasted_iota(jnp.int32, (blk, blk), 0)
    col = lax.broadcasted_iota(jnp.int32, (blk, blk), 1)
    causal = col <= row

    for i in range(nb):
        rows = slice(i * blk, (i + 1) * blk)
        qp = q_ref[0, rows, :].astype(F32)
        for hd in range(2):
            q0 = jnp.where(in_head[hd], qp, 0.0).astype(BF16)
            if i == 0:
                q_ext[hd, rows, :] = q0
                continue
            bias = _topk_bias(_dot_nt(q0, km[hd]), i, bias_base[hd])
            q_ext[hd, rows, :] = jnp.where(in_head[hd], qp, bias).astype(BF16)

    for i in range(nb):
        length = (i + 1) * blk
        outs = []
        for hd in range(2):
            q_tile = q_ext[hd, i * blk:(i + 1) * blk, :]
            mrun = None
            for c0 in range(0, length, chunk):
                cw = min(chunk, length - c0)
                s = _dot_nt(q_tile, k_ext[hd, c0:c0 + cw, :])
                if c0 + cw == length:
                    own = jnp.where(causal, s[:, cw - blk:], NEG)
                    s = own if cw == blk else jnp.concatenate([s[:, :cw - blk], own], axis=1)
                s_scr[hd, :, c0:c0 + cw] = s
                for l0 in range(0, cw, LANES):
                    part = s[:, l0:l0 + LANES]
                    mrun = part if mrun is None else jnp.maximum(mrun, part)
            m = jnp.max(mrun, axis=-1, keepdims=True)
            for c0 in range(0, length, chunk):
                cw = min(chunk, length - c0)
                p_scr[hd, :, c0:c0 + cw] = jnp.exp2(s_scr[hd, :, c0:c0 + cw] - m).astype(BF16)
            outs.append(_dot(p_scr[hd, :, 0:length], v_ext[hd, 0:length, :]))
        num = jnp.where(in_head[0], outs[0], outs[1])
        den = pltpu.roll(jnp.where(in_head[0], outs[1], outs[0]), half, axis=1)
        o_ref[0, i * blk:(i + 1) * blk, :] = (num / den).astype(o_ref.dtype)


def _moba(q, k, v, chunk):
    bsz, s, aw = q.shape
    spec = pl.BlockSpec((1, s, LANES), lambda b, p: (b, 0, p))
    return pl.pallas_call(
        functools.partial(_moba_kernel, chunk=chunk),
        grid=(bsz, aw // LANES),
        in_specs=[spec, spec, spec],
        out_specs=spec,
        out_shape=jax.ShapeDtypeStruct((bsz, s, aw), BF16),
        scratch_shapes=[
            pltpu.VMEM((2, s, LANES), BF16),
            pltpu.VMEM((2, s, LANES), BF16),
            pltpu.VMEM((2, s, LANES), BF16),
            pltpu.VMEM((2, MOBA_BLOCK, s), F32),
            pltpu.VMEM((2, MOBA_BLOCK, s), BF16),
        ],
        compiler_params=pltpu.CompilerParams(
            dimension_semantics=("parallel", "parallel"), vmem_limit_bytes=VMEM_LIMIT_BYTES),
        name="moba",
    )(q, k, v)


def _mix_out_kernel(attn_ref, sgu_ref, x_ref, mod_ref, g_attn_ref, w_out_ref, g_post_ref, o_ref):
    aw = attn_ref.shape[2]
    a = _rms(attn_ref[0].astype(F32), g_attn_ref[...]).astype(BF16)
    mix = _dot(a, w_out_ref[0:aw, :]) + _dot(sgu_ref[0], w_out_ref[aw:, :])
    gate = mod_ref[0, 2:3, :]
    o_ref[0] = x_ref[0] + gate * _rms(mix, g_post_ref[...])


def _mix_out(attn, sgu, x, mod, g_attn, w_out, g_post, tm):
    bsz, s, d = x.shape
    aw = attn.shape[2]
    row_spec = lambda w: pl.BlockSpec((1, tm, w), lambda b, t: (b, t, 0))
    full = lambda shape: pl.BlockSpec(shape, lambda b, t: (0,) * len(shape))
    return pl.pallas_call(
        _mix_out_kernel,
        grid=(bsz, s // tm),
        in_specs=[
            row_spec(aw), row_spec(sgu.shape[2]), row_spec(d),
            pl.BlockSpec((1, N_MOD, d), lambda b, t: (b, 0, 0)),
            full((1, aw)), full(w_out.shape), full((1, d)),
        ],
        out_specs=row_spec(d),
        out_shape=jax.ShapeDtypeStruct((bsz, s, d), F32),
        compiler_params=pltpu.CompilerParams(
            dimension_semantics=("parallel", "parallel"), vmem_limit_bytes=VMEM_LIMIT_BYTES),
        name="mix_out",
    )(attn, sgu, x, mod, g_attn, w_out, g_post)


def _ffn_kernel(x_ref, mod_ref, g_pre_ref, w_gu_ref, w_down_ref, g_post_ref, o_ref, act_scr, *,
                chunk):
    d_ff = w_down_ref.shape[0]
    x = x_ref[0]
    shift = mod_ref[0, 3:4, :]
    scale = mod_ref[0, 4:5, :]
    gate = mod_ref[0, 5:6, :]
    h = (_rms(x, g_pre_ref[...]) * (1.0 + scale) + shift).astype(BF16)
    for c0 in range(0, d_ff, chunk):
        g = _dot(h, w_gu_ref[:, c0:c0 + chunk])
        up = _dot(h, w_gu_ref[:, d_ff + c0:d_ff + c0 + chunk])
        act_scr[:, c0:c0 + chunk] = (g * jax.nn.sigmoid(g) * up).astype(BF16)
    y = _dot(act_scr[...], w_down_ref[...])
    o_ref[0] = x + gate * _rms(y, g_post_ref[...])


def _ffn(x, mod, g_pre, w_gu, w_down, g_post, tm, chunk):
    bsz, s, d = x.shape
    d_ff = w_down.shape[0]
    row_spec = pl.BlockSpec((1, tm, d), lambda b, t: (b, t, 0))
    full = lambda shape: pl.BlockSpec(shape, lambda b, t: (0,) * len(shape))
    return pl.pallas_call(
        functools.partial(_ffn_kernel, chunk=chunk),
        grid=(bsz, s // tm),
        in_specs=[
            row_spec,
            pl.BlockSpec((1, N_MOD, d), lambda b, t: (b, 0, 0)),
            full((1, d)), full(w_gu.shape), full(w_down.shape), full((1, d)),
        ],
        out_specs=row_spec,
        out_shape=jax.ShapeDtypeStruct((bsz, s, d), F32),
        scratch_shapes=[pltpu.VMEM((tm, d_ff), BF16)],
        compiler_params=pltpu.CompilerParams(
            dimension_semantics=("parallel", "parallel"), vmem_limit_bytes=VMEM_LIMIT_BYTES),
        name="ffn",
    )(x, mod, g_pre, w_gu, w_down, g_post)


def kernel(x, c, w_ada, b_ada, g_pre_mix, g_post_mix, w_in, g_sgu_norm, w_sgu, b_sgu,
           g_attn_out, g_sgu_out, w_out, g_pre_ffn, g_post_ffn, w_gate_up, w_down):
    bsz, s, d = x.shape
    depth = w_ada.shape[0]
    d_ff = w_down.shape[1]
    assert s % MOBA_BLOCK == 0 and s // MOBA_BLOCK <= HEAD_DIM
    assert w_in.shape[2] == 3 * ATTN_WIDTH + 2 * N_SGU_GROUPS * LANES
    tm_in, tm_mix, tm_ffn = 512, 512, 512
    ffn_chunk = 256
    moba_chunk = 2 * MOBA_BLOCK
    assert d_ff % ffn_chunk == 0
    row = lambda a: a.reshape(1, -1)
    for l in range(depth):
        mod = _ada(c, w_ada[l], b_ada[l]).reshape(bsz, N_MOD, d)
        q, k, v, sgu = _in_proj(
            x, mod, row(g_pre_mix[l]), w_in[l].astype(BF16), row(g_sgu_norm[l]), w_sgu[l],
            b_sgu[l].T, row(g_sgu_out[l]), tm_in)
        attn = _moba(q, k, v, moba_chunk)
        x = _mix_out(attn, sgu, x, mod, row(g_attn_out[l]), w_out[l].astype(BF16),
                     row(g_post_mix[l]), tm_mix)
        x = _ffn(x, mod, row(g_pre_ffn[l]), w_gate_up[l].astype(BF16), w_down[l].astype(BF16),
                 row(g_post_ffn[l]), tm_ffn, ffn_chunk)
    return x
```

```python
import functools
import math

import jax
import jax.numpy as jnp
from jax import lax
from jax.experimental import pallas as pl
from jax.experimental.pallas import tpu as pltpu

N_HEADS = 8
HEAD_DIM = 64
ATTN_WIDTH = N_HEADS * HEAD_DIM
N_SGU_GROUPS = 4
SGU_CHUNK = 128
MOBA_BLOCK = 256
MOBA_TOPK = 3
N_MOD = 6
EPS = 1e-6
NEG = -1e30

LANES = 128
VMEM_LIMIT_BYTES = 56 * 1024 * 1024
Q_SCALE = HEAD_DIM ** -0.5 * math.log2(math.e)

BF16 = jnp.bfloat16
F32 = jnp.float32


def _rms(x, g):
    return x * lax.rsqrt(jnp.mean(x * x, axis=-1, keepdims=True) + EPS) * g


def _dot(a, b):
    return jnp.dot(a, b, preferred_element_type=F32)


def _dot_nt(a, b):
    return lax.dot_general(a, b, (((1,), (1,)), ((), ())), preferred_element_type=F32)


def _ada_kernel(c_ref, w_ref, b_ref, o_ref):
    c = c_ref[...]
    c_act = (c * jax.nn.sigmoid(c)).astype(BF16)
    o_ref[...] = _dot(c_act, w_ref[...].astype(BF16)) + b_ref[...]


def _ada(c, w_ada, b_ada):
    bsz, d = c.shape
    n = w_ada.shape[1]
    tn = d
    return pl.pallas_call(
        _ada_kernel,
        grid=(n // tn,),
        in_specs=[
            pl.BlockSpec((bsz, d), lambda j: (0, 0)),
            pl.BlockSpec((d, tn), lambda j: (0, j)),
            pl.BlockSpec((1, tn), lambda j: (0, j)),
        ],
        out_specs=pl.BlockSpec((bsz, tn), lambda j: (0, j)),
        out_shape=jax.ShapeDtypeStruct((bsz, n), F32),
        compiler_params=pltpu.CompilerParams(
            dimension_semantics=("arbitrary",), vmem_limit_bytes=VMEM_LIMIT_BYTES),
        name="ada",
    )(c, w_ada, b_ada.reshape(1, n))


def _in_proj_kernel(x_ref, mod_ref, g_pre_ref, w_in_ref, g_ln_ref, w_sgu_ref, b_sgu_ref,
                    g_sgu_out_ref, q_ref, k_ref, v_ref, sgu_ref, sgu_scr):
    tm = x_ref.shape[1]
    aw = ATTN_WIDTH
    gd = LANES
    n_chunks = tm // SGU_CHUNK

    x = x_ref[0]
    shift = mod_ref[0, 0:1, :]
    scale = mod_ref[0, 1:2, :]
    h = _rms(x, g_pre_ref[...]) * (1.0 + scale) + shift
    proj = _dot(h.astype(BF16), w_in_ref[...])

    q_ref[0] = (proj[:, 0:aw] * Q_SCALE).astype(BF16)
    k_ref[0] = proj[:, aw:2 * aw].astype(BF16)
    v_ref[0] = proj[:, 2 * aw:3 * aw].astype(BF16)

    u = jax.nn.gelu(proj[:, 3 * aw:3 * aw + N_SGU_GROUPS * gd])
    vg = jax.nn.gelu(proj[:, 3 * aw + N_SGU_GROUPS * gd:])

    row = lax.broadcasted_iota(jnp.int32, (SGU_CHUNK, SGU_CHUNK), 0)
    col = lax.broadcasted_iota(jnp.int32, (SGU_CHUNK, SGU_CHUNK), 1)
    causal = col <= row
    for g in range(N_SGU_GROUPS):
        lo = g * gd
        vgg = vg[:, lo:lo + gd]
        d = vgg - jnp.mean(vgg, axis=-1, keepdims=True)
        yn = d * lax.rsqrt(jnp.mean(d * d, axis=-1, keepdims=True) + EPS) * g_ln_ref[:, lo:lo + gd]
        yn = yn.astype(BF16)
        rhs = jnp.concatenate(
            [yn[n * SGU_CHUNK:(n + 1) * SGU_CHUNK, :] for n in range(n_chunks)], axis=1)
        w = jnp.where(causal, w_sgu_ref[g], 0.0).astype(BF16)
        mixed = _dot(w, rhs) + b_sgu_ref[:, g:g + 1]
        for n in range(n_chunks):
            r0 = n * SGU_CHUNK
            sgu_scr[r0:r0 + SGU_CHUNK, lo:lo + gd] = (
                u[r0:r0 + SGU_CHUNK, lo:lo + gd] * mixed[:, n * gd:(n + 1) * gd])
    sgu_ref[0] = _rms(sgu_scr[...], g_sgu_out_ref[...]).astype(BF16)


def _in_proj(x, mod, g_pre, w_in, g_ln, w_sgu, b_sgu_t, g_sgu_out, tm):
    bsz, s, d = x.shape
    in_w = w_in.shape[1]
    aw = ATTN_WIDTH
    sw = in_w - 3 * aw
    sgu_w = sw // 2
    row_spec = lambda w: pl.BlockSpec((1, tm, w), lambda b, t: (b, t, 0))
    full = lambda shape: pl.BlockSpec(shape, lambda b, t: (0,) * len(shape),
                                      pipeline_mode=pl.Buffered(1))
    out_sds = jax.ShapeDtypeStruct((bsz, s, aw), BF16)
    return pl.pallas_call(
        _in_proj_kernel,
        grid=(bsz, s // tm),
        in_specs=[
            row_spec(d),
            pl.BlockSpec((1, N_MOD, d), lambda b, t: (b, 0, 0)),
            full((1, d)),
            full((d, in_w)),
            full((1, sgu_w)),
            full((N_SGU_GROUPS, SGU_CHUNK, SGU_CHUNK)),
            full((SGU_CHUNK, N_SGU_GROUPS)),
            full((1, sgu_w)),
        ],
        out_specs=[row_spec(aw), row_spec(aw), row_spec(aw), row_spec(sgu_w)],
        out_shape=[out_sds, out_sds, out_sds, jax.ShapeDtypeStruct((bsz, s, sgu_w), BF16)],
        scratch_shapes=[pltpu.VMEM((tm, sgu_w), F32)],
        compiler_params=pltpu.CompilerParams(
            dimension_semantics=("parallel", "parallel"), vmem_limit_bytes=VMEM_LIMIT_BYTES),
        name="in_proj",
    )(x, mod, g_pre, w_in, g_ln, w_sgu, b_sgu_t, g_sgu_out)


def _topk_bias(gate, n_past, base):
    t = gate.shape[0]
    rows = -(-n_past // 8) * 8
    g = gate.T[base:base + rows, :]
    nrow = lax.broadcasted_iota(jnp.int32, (rows, t), 0)
    beaten = jnp.zeros((rows, t), F32)
    for other in range(n_past):
        b = g[other:other + 1, :]
        beaten = beaten + jnp.where(b > g, 1.0, 0.0)
        beaten = beaten + jnp.where(jnp.logical_and(b == g, nrow > other), 1.0, 0.0)
    bias_t = jnp.where(nrow < n_past, jnp.where(beaten < MOBA_TOPK, 0.0, NEG), 0.0)
    pieces = [bias_t]
    if base:
        pieces.insert(0, jnp.zeros((base, t), F32))
    if base + rows < LANES:
        pieces.append(jnp.zeros((LANES - base - rows, t), F32))
    return jnp.concatenate(pieces, axis=0).T


def _moba_kernel(q_ref, k_ref, v_ref, o_ref, q_ext, k_ext, v_ext, s_scr, p_scr, *, chunk):
    seq = k_ref.shape[1]
    blk = MOBA_BLOCK
    nb = seq // blk
    half = HEAD_DIM
    lane = lax.broadcasted_iota(jnp.int32, (blk, LANES), 1)
    in_head = (lane < half, lane >= half)
    bias_base = (half, 0)

    for j in range(nb):
        rows = slice(j * blk, (j + 1) * blk)
        kb = k_ref[0, rows, :].astype(F32)
        vb = v_ref[0, rows, :].astype(F32)
        for hd in range(2):
            onehot = jnp.where(lane == bias_base[hd] + j, 1.0, 0.0)
            k_ext[hd, rows, :] = jnp.where(in_head[hd], kb, onehot).astype(BF16)
            v_ext[hd, rows, :] = jnp.where(in_head[hd], vb, 1.0).astype(BF16)

    prow = lax.broadcasted_iota(jnp.int32, (LANES, seq), 0)
    pcol = lax.broadcasted_iota(jnp.int32, (LANES, seq), 1)
    pr = jnp.bitwise_and(prow, half - 1)
    pblk = lax.shift_right_logical(pcol, blk.bit_length() - 1)
    psel = jnp.where(jnp.logical_and(pr == pblk, pr < nb), 1.0 / blk, 0.0).astype(BF16)
    km_full = _dot(psel, k_ref[0])
    krow = lax.broadcasted_iota(jnp.int32, (LANES, LANES), 0)
    klane = lax.broadcasted_iota(jnp.int32, (LANES, LANES), 1)
    km = (
        jnp.where(jnp.logical_and(krow >= half, klane < half), km_full, 0.0).astype(BF16),
        jnp.where(jnp.logical_and(krow < half, klane >= half), km_full, 0.0).astype(BF16),
    )

    row = lax.broadcasted_iota(jnp.int32, (blk, blk), 0)
    col = lax.broadcasted_iota(jnp.int32, (blk, blk), 1)
    causal = col <= row

    for i in range(nb):
        rows = slice(i * blk, (i + 1) * blk)
        qp = q_ref[0, rows, :].astype(F32)
        for hd in range(2):
            q0 = jnp.where(in_head[hd], qp, 0.0).astype(BF16)
            if i == 0:
                q_ext[hd, rows, :] = q0
                continue
            bias = _topk_bias(_dot_nt(q0, km[hd]), i, bias_base[hd])
            q_ext[hd, rows, :] = jnp.where(in_head[hd], qp, bias).astype(BF16)

    for i in range(nb):
        length = (i + 1) * blk
        outs = []
        for hd in range(2):
            q_tile = q_ext[hd, i * blk:(i + 1) * blk, :]
            mrun = None
            for c0 in range(0, length, chunk):
                cw = min(chunk, length - c0)
                s = _dot_nt(q_tile, k_ext[hd, c0:c0 + cw, :])
                if c0 + cw == length:
                    own = jnp.where(causal, s[:, cw - blk:], NEG)
                    s = own if cw == blk else jnp.concatenate([s[:, :cw - blk], own], axis=1)
                s_scr[hd, :, c0:c0 + cw] = s
                for l0 in range(0, cw, LANES):
                    part = s[:, l0:l0 + LANES]
                    mrun = part if mrun is None else jnp.maximum(mrun, part)
            m = jnp.max(mrun, axis=-1, keepdims=True)
            for c0 in range(0, length, chunk):
                cw = min(chunk, length - c0)
                p_scr[hd, :, c0:c0 + cw] = jnp.exp2(s_scr[hd, :, c0:c0 + cw] - m).astype(BF16)
            outs.append(_dot(p_scr[hd, :, 0:length], v_ext[hd, 0:length, :]))
        num = jnp.where(in_head[0], outs[0], outs[1])
        den = pltpu.roll(jnp.where(in_head[0], outs[1], outs[0]), half, axis=1)
        o_ref[0, i * blk:(i + 1) * blk, :] = (num / den).astype(o_ref.dtype)


def _moba(q, k, v, chunk):
    bsz, s, aw = q.shape
    spec = pl.BlockSpec((1, s, LANES), lambda b, p: (b, 0, p))
    return pl.pallas_call(
        functools.partial(_moba_kernel, chunk=chunk),
        grid=(bsz, aw // LANES),
        in_specs=[spec, spec, spec],
        out_specs=spec,
        out_shape=jax.ShapeDtypeStruct((bsz, s, aw), BF16),
        scratch_shapes=[
            pltpu.VMEM((2, s, LANES), BF16),
            pltpu.VMEM((2, s, LANES), BF16),
            pltpu.VMEM((2, s, LANES), BF16),
            pltpu.VMEM((2, MOBA_BLOCK, s), F32),
            pltpu.VMEM((2, MOBA_BLOCK, s), BF16),
        ],
        compiler_params=pltpu.CompilerParams(
            dimension_semantics=("parallel", "parallel"), vmem_limit_bytes=VMEM_LIMIT_BYTES),
        name="moba",
    )(q, k, v)


def _mix_ffn_kernel(attn_ref, sgu_ref, x_ref, mod_ref, g_attn_ref, w_out_ref, g_post_mix_ref,
                    g_pre_ref, w_gu_ref, w_down_ref, g_post_ref, o_ref, act_scr, *, chunk):
    aw = attn_ref.shape[2]
    d_ff = w_down_ref.shape[0]
    gate_m = mod_ref[0, 2:3, :]
    shift = mod_ref[0, 3:4, :]
    scale = mod_ref[0, 4:5, :]
    gate_f = mod_ref[0, 5:6, :]

    a = _rms(attn_ref[0].astype(F32), g_attn_ref[...]).astype(BF16)
    mix = _dot(a, w_out_ref[0:aw, :]) + _dot(sgu_ref[0], w_out_ref[aw:, :])
    x = x_ref[0] + gate_m * _rms(mix, g_post_mix_ref[...])

    h = (_rms(x, g_pre_ref[...]) * (1.0 + scale) + shift).astype(BF16)
    for c0 in range(0, d_ff, chunk):
        g = _dot(h, w_gu_ref[:, c0:c0 + chunk])
        up = _dot(h, w_gu_ref[:, d_ff + c0:d_ff + c0 + chunk])
        act_scr[:, c0:c0 + chunk] = (g * jax.nn.sigmoid(g) * up).astype(BF16)
    y = _dot(act_scr[...], w_down_ref[...])
    o_ref[0] = x + gate_f * _rms(y, g_post_ref[...])


def _mix_ffn(attn, sgu, x, mod, g_attn, w_out, g_post_mix, g_pre, w_gu, w_down, g_post, tm, chunk):
    bsz, s, d = x.shape
    aw = attn.shape[2]
    d_ff = w_down.shape[0]
    row_spec = lambda w: pl.BlockSpec((1, tm, w), lambda b, t: (b, t, 0))
    full = lambda shape: pl.BlockSpec(shape, lambda b, t: (0,) * len(shape),
                                      pipeline_mode=pl.Buffered(1))
    return pl.pallas_call(
        functools.partial(_mix_ffn_kernel, chunk=chunk),
        grid=(bsz, s // tm),
        in_specs=[
            row_spec(aw), row_spec(sgu.shape[2]), row_spec(d),
            pl.BlockSpec((1, N_MOD, d), lambda b, t: (b, 0, 0)),
            full((1, aw)), full(w_out.shape), full((1, d)),
            full((1, d)), full(w_gu.shape), full(w_down.shape), full((1, d)),
        ],
        out_specs=row_spec(d),
        out_shape=jax.ShapeDtypeStruct((bsz, s, d), F32),
        scratch_shapes=[pltpu.VMEM((tm, d_ff), BF16)],
        compiler_params=pltpu.CompilerParams(
            dimension_semantics=("parallel", "parallel"), vmem_limit_bytes=VMEM_LIMIT_BYTES),
        name="mix_ffn",
    )(attn, sgu, x, mod, g_attn, w_out, g_post_mix, g_pre, w_gu, w_down, g_post)


def kernel(x, c, w_ada, b_ada, g_pre_mix, g_post_mix, w_in, g_sgu_norm, w_sgu, b_sgu,
           g_attn_out, g_sgu_out, w_out, g_pre_ffn, g_post_ffn, w_gate_up, w_down):
    bsz, s, d = x.shape
    depth = w_ada.shape[0]
    d_ff = w_down.shape[1]
    assert s % MOBA_BLOCK == 0 and s // MOBA_BLOCK <= HEAD_DIM
    assert w_in.shape[2] == 3 * ATTN_WIDTH + 2 * N_SGU_GROUPS * LANES
    tm_in, tm_ffn = 512, 512
    ffn_chunk = 256
    moba_chunk = 2 * MOBA_BLOCK
    assert d_ff % ffn_chunk == 0
    row = lambda a: a.reshape(1, -1)
    for l in range(depth):
        mod = _ada(c, w_ada[l], b_ada[l]).reshape(bsz, N_MOD, d)
        q, k, v, sgu = _in_proj(
            x, mod, row(g_pre_mix[l]), w_in[l].astype(BF16), row(g_sgu_norm[l]), w_sgu[l],
            b_sgu[l].T, row(g_sgu_out[l]), tm_in)
        attn = _moba(q, k, v, moba_chunk)
        x = _mix_ffn(attn, sgu, x, mod, row(g_attn_out[l]), w_out[l].astype(BF16),
                     row(g_post_mix[l]), row(g_pre_ffn[l]), w_gate_up[l].astype(BF16),
                     w_down[l].astype(BF16), row(g_post_ffn[l]), tm_ffn, ffn_chunk)
    return x
```

```python
import functools
import math

import jax
import jax.numpy as jnp
from jax import lax
from jax.experimental import pallas as pl
from jax.experimental.pallas import tpu as pltpu

N_HEADS = 8
HEAD_DIM = 64
ATTN_WIDTH = N_HEADS * HEAD_DIM
N_SGU_GROUPS = 4
SGU_CHUNK = 128
MOBA_BLOCK = 256
MOBA_TOPK = 3
N_MOD = 6
EPS = 1e-6
NEG = -1e30

LANES = 128
VMEM_LIMIT_BYTES = 56 * 1024 * 1024
Q_SCALE = HEAD_DIM ** -0.5 * math.log2(math.e)

BF16 = jnp.bfloat16
F32 = jnp.float32


def _rms(x, g):
    return x * lax.rsqrt(jnp.mean(x * x, axis=-1, keepdims=True) + EPS) * g


def _dot(a, b):
    return jnp.dot(a, b, preferred_element_type=F32)


def _dot_nt(a, b):
    return lax.dot_general(a, b, (((1,), (1,)), ((), ())), preferred_element_type=F32)


def _ada_kernel(c_ref, w_ref, b_ref, o_ref):
    c = c_ref[...]
    c_act = (c * jax.nn.sigmoid(c)).astype(BF16)
    o_ref[...] = _dot(c_act, w_ref[...].astype(BF16)) + b_ref[...]


def _ada(c, w_ada, b_ada):
    bsz, d = c.shape
    n = w_ada.shape[1]
    tn = d
    return pl.pallas_call(
        _ada_kernel,
        grid=(n // tn,),
        in_specs=[
            pl.BlockSpec((bsz, d), lambda j: (0, 0)),
            pl.BlockSpec((d, tn), lambda j: (0, j)),
            pl.BlockSpec((1, tn), lambda j: (0, j)),
        ],
        out_specs=pl.BlockSpec((bsz, tn), lambda j: (0, j)),
        out_shape=jax.ShapeDtypeStruct((bsz, n), F32),
        compiler_params=pltpu.CompilerParams(
            dimension_semantics=("arbitrary",), vmem_limit_bytes=VMEM_LIMIT_BYTES),
        name="ada",
    )(c, w_ada, b_ada.reshape(1, n))


def _in_proj_kernel(x_ref, mod_ref, g_pre_ref, w_in_ref, g_ln_ref, w_sgu_ref, b_sgu_ref,
                    g_sgu_out_ref, q_ref, k_ref, v_ref, sgu_ref, sgu_scr):
    tm = x_ref.shape[1]
    aw = ATTN_WIDTH
    gd = LANES
    n_chunks = tm // SGU_CHUNK

    x = x_ref[0]
    shift = mod_ref[0, 0:1, :]
    scale = mod_ref[0, 1:2, :]
    h = _rms(x, g_pre_ref[...]) * (1.0 + scale) + shift
    proj = _dot(h.astype(BF16), w_in_ref[...])

    q_ref[0] = (proj[:, 0:aw] * Q_SCALE).astype(BF16)
    k_ref[0] = proj[:, aw:2 * aw].astype(BF16)
    v_ref[0] = proj[:, 2 * aw:3 * aw].astype(BF16)

    u = jax.nn.gelu(proj[:, 3 * aw:3 * aw + N_SGU_GROUPS * gd])
    vg = jax.nn.gelu(proj[:, 3 * aw + N_SGU_GROUPS * gd:])

    row = lax.broadcasted_iota(jnp.int32, (SGU_CHUNK, SGU_CHUNK), 0)
    col = lax.broadcasted_iota(jnp.int32, (SGU_CHUNK, SGU_CHUNK), 1)
    causal = col <= row
    for g in range(N_SGU_GROUPS):
        lo = g * gd
        vgg = vg[:, lo:lo + gd]
        d = vgg - jnp.mean(vgg, axis=-1, keepdims=True)
        yn = d * lax.rsqrt(jnp.mean(d * d, axis=-1, keepdims=True) + EPS) * g_ln_ref[:, lo:lo + gd]
        yn = yn.astype(BF16)
        rhs = jnp.concatenate(
            [yn[n * SGU_CHUNK:(n + 1) * SGU_CHUNK, :] for n in range(n_chunks)], axis=1)
        w = jnp.where(causal, w_sgu_ref[g], 0.0).astype(BF16)
        mixed = _dot(w, rhs) + b_sgu_ref[:, g:g + 1]
        for n in range(n_chunks):
            r0 = n * SGU_CHUNK
            sgu_scr[r0:r0 + SGU_CHUNK, lo:lo + gd] = (
                u[r0:r0 + SGU_CHUNK, lo:lo + gd] * mixed[:, n * gd:(n + 1) * gd])
    sgu_ref[0] = _rms(sgu_scr[...], g_sgu_out_ref[...]).astype(BF16)


def _in_proj(x, mod, g_pre, w_in, g_ln, w_sgu, b_sgu_t, g_sgu_out, tm):
    bsz, s, d = x.shape
    in_w = w_in.shape[1]
    aw = ATTN_WIDTH
    sw = in_w - 3 * aw
    sgu_w = sw // 2
    row_spec = lambda w: pl.BlockSpec((1, tm, w), lambda b, t: (b, t, 0))
    full = lambda shape: pl.BlockSpec(shape, lambda b, t: (0,) * len(shape),
                                      pipeline_mode=pl.Buffered(1))
    out_sds = jax.ShapeDtypeStruct((bsz, s, aw), BF16)
    return pl.pallas_call(
        _in_proj_kernel,
        grid=(bsz, s // tm),
        in_specs=[
            row_spec(d),
            pl.BlockSpec((1, N_MOD, d), lambda b, t: (b, 0, 0)),
            full((1, d)),
            full((d, in_w)),
            full((1, sgu_w)),
            full((N_SGU_GROUPS, SGU_CHUNK, SGU_CHUNK)),
            full((SGU_CHUNK, N_SGU_GROUPS)),
            full((1, sgu_w)),
        ],
        out_specs=[row_spec(aw), row_spec(aw), row_spec(aw), row_spec(sgu_w)],
        out_shape=[out_sds, out_sds, out_sds, jax.ShapeDtypeStruct((bsz, s, sgu_w), BF16)],
        scratch_shapes=[pltpu.VMEM((tm, sgu_w), F32)],
        compiler_params=pltpu.CompilerParams(
            dimension_semantics=("parallel", "parallel"), vmem_limit_bytes=VMEM_LIMIT_BYTES),
        name="in_proj",
    )(x, mod, g_pre, w_in, g_ln, w_sgu, b_sgu_t, g_sgu_out)


def _topk_bias(gate, n_past, base):
    t = gate.shape[0]
    rows = -(-n_past // 8) * 8
    g = gate.T[base:base + rows, :]
    nrow = lax.broadcasted_iota(jnp.int32, (rows, t), 0)
    beaten = jnp.zeros((rows, t), F32)
    for other in range(n_past):
        b = g[other:other + 1, :]
        beaten = beaten + jnp.where(b > g, 1.0, 0.0)
        beaten = beaten + jnp.where(jnp.logical_and(b == g, nrow > other), 1.0, 0.0)
    bias_t = jnp.where(nrow < n_past, jnp.where(beaten < MOBA_TOPK, 0.0, NEG), 0.0)
    pieces = [bias_t]
    if base:
        pieces.insert(0, jnp.zeros((base, t), F32))
    if base + rows < LANES:
        pieces.append(jnp.zeros((LANES - base - rows, t), F32))
    return jnp.concatenate(pieces, axis=0).T


def _moba_kernel(q_ref, k_ref, v_ref, o_ref, q_ext, k_ext, vt_ext, *, chunk):
    seq = k_ref.shape[1]
    blk = MOBA_BLOCK
    nb = seq // blk
    half = HEAD_DIM
    lane = lax.broadcasted_iota(jnp.int32, (blk, LANES), 1)
    in_head = (lane < half, lane >= half)
    bias_base = (half, 0)

    trow = lax.broadcasted_iota(jnp.int32, (LANES, blk), 0)
    in_head_t = (trow < half, trow >= half)

    for j in range(nb):
        rows = slice(j * blk, (j + 1) * blk)
        kb = k_ref[0, rows, :].astype(F32)
        vb_t = v_ref[0, rows, :].astype(F32).T
        for hd in range(2):
            onehot = jnp.where(lane == bias_base[hd] + j, 1.0, 0.0)
            k_ext[hd, rows, :] = jnp.where(in_head[hd], kb, onehot).astype(BF16)
            vt_ext[hd, :, rows] = jnp.where(in_head_t[hd], vb_t, 1.0).astype(BF16)

    prow = lax.broadcasted_iota(jnp.int32, (LANES, seq), 0)
    pcol = lax.broadcasted_iota(jnp.int32, (LANES, seq), 1)
    pr = jnp.bitwise_and(prow, half - 1)
    pblk = lax.shift_right_logical(pcol, blk.bit_length() - 1)
    psel = jnp.where(jnp.logical_and(pr == pblk, pr < nb), 1.0 / blk, 0.0).astype(BF16)
    km_full = _dot(psel, k_ref[0])
    krow = lax.broadcasted_iota(jnp.int32, (LANES, LANES), 0)
    klane = lax.broadcasted_iota(jnp.int32, (LANES, LANES), 1)
    km = (
        jnp.where(jnp.logical_and(krow >= half, klane < half), km_full, 0.0).astype(BF16),
        jnp.where(jnp.logical_and(krow < half, klane >= half), km_full, 0.0).astype(BF16),
    )

    key_pos = lax.broadcasted_iota(jnp.int32, (blk, blk), 0)
    q_pos = lax.broadcasted_iota(jnp.int32, (blk, blk), 1)
    causal = key_pos <= q_pos

    for i in range(nb):
        rows = slice(i * blk, (i + 1) * blk)
        qp = q_ref[0, rows, :].astype(F32)
        for hd in range(2):
            q0 = jnp.where(in_head[hd], qp, 0.0).astype(BF16)
            if i == 0:
                q_ext[hd, rows, :] = q0
                continue
            bias = _topk_bias(_dot_nt(q0, km[hd]), i, bias_base[hd])
            q_ext[hd, rows, :] = jnp.where(in_head[hd], qp, bias).astype(BF16)

    items = [(i, max(c1 - chunk, 0), c1)
             for i in range(nb) for c1 in range((i + 1) * blk, 0, -chunk)]

    def scores(item):
        i, c0, c1 = item
        return [_dot_nt(k_ext[hd, c0:c1, :], q_ext[hd, i * blk:(i + 1) * blk, :])
                for hd in range(2)]

    s_next = scores(items[0])
    m = [None, None]
    acc = [None, None]
    for n, (i, c0, c1) in enumerate(items):
        s_cur = s_next
        if n + 1 < len(items):
            s_next = scores(items[n + 1])
        cw = c1 - c0
        for hd in range(2):
            s = s_cur[hd]
            if c1 == (i + 1) * blk:
                own = jnp.where(causal, s[cw - blk:, :], NEG)
                s = own if cw == blk else jnp.concatenate([s[:cw - blk, :], own], axis=0)
            m_chunk = jnp.max(jnp.max(s.reshape(cw // 8, 8, blk), axis=0), axis=0, keepdims=True)
            m_new = m_chunk if m[hd] is None else jnp.maximum(m[hd], m_chunk)
            pv = _dot(vt_ext[hd, :, c0:c1], jnp.exp2(s - m_new).astype(BF16))
            acc[hd] = pv if acc[hd] is None else acc[hd] * jnp.exp2(m[hd] - m_new) + pv
            m[hd] = m_new
        if c0 == 0:
            num = jnp.where(in_head_t[0], acc[0], acc[1])
            inv = jnp.where(in_head_t[0], 1.0 / acc[0][half:half + 1, :], 1.0 / acc[1][0:1, :])
            o_ref[0, i * blk:(i + 1) * blk, :] = (num * inv).T.astype(o_ref.dtype)
            m = [None, None]
            acc = [None, None]


def _moba(q, k, v, chunk):
    bsz, s, aw = q.shape
    spec = pl.BlockSpec((1, s, LANES), lambda b, p: (b, 0, p))
    return pl.pallas_call(
        functools.partial(_moba_kernel, chunk=chunk),
        grid=(bsz, aw // LANES),
        in_specs=[spec, spec, spec],
        out_specs=spec,
        out_shape=jax.ShapeDtypeStruct((bsz, s, aw), BF16),
        scratch_shapes=[
            pltpu.VMEM((2, s, LANES), BF16),
            pltpu.VMEM((2, s, LANES), BF16),
            pltpu.VMEM((2, LANES, s), BF16),
        ],
        compiler_params=pltpu.CompilerParams(
            dimension_semantics=("parallel", "parallel"), vmem_limit_bytes=VMEM_LIMIT_BYTES),
        name="moba",
    )(q, k, v)


def _mix_ffn_kernel(attn_ref, sgu_ref, x_ref, mod_ref, g_attn_ref, w_out_ref, g_post_mix_ref,
                    g_pre_ref, w_gu_ref, w_down_ref, g_post_ref, o_ref, act_scr, *, chunk):
    aw = attn_ref.shape[2]
    d_ff = w_down_ref.shape[0]
    gate_m = mod_ref[0, 2:3, :]
    shift = mod_ref[0, 3:4, :]
    scale = mod_ref[0, 4:5, :]
    gate_f = mod_ref[0, 5:6, :]

    a = _rms(attn_ref[0].astype(F32), g_attn_ref[...]).astype(BF16)
    mix = _dot(a, w_out_ref[0:aw, :]) + _dot(sgu_ref[0], w_out_ref[aw:, :])
    x = x_ref[0] + gate_m * _rms(mix, g_post_mix_ref[...])

    h = (_rms(x, g_pre_ref[...]) * (1.0 + scale) + shift).astype(BF16)
    for c0 in range(0, d_ff, chunk):
        g = _dot(h, w_gu_ref[:, c0:c0 + chunk])
        up = _dot(h, w_gu_ref[:, d_ff + c0:d_ff + c0 + chunk])
        act_scr[:, c0:c0 + chunk] = (g * jax.nn.sigmoid(g) * up).astype(BF16)
    y = _dot(act_scr[...], w_down_ref[...])
    o_ref[0] = x + gate_f * _rms(y, g_post_ref[...])


def _mix_ffn(attn, sgu, x, mod, g_attn, w_out, g_post_mix, g_pre, w_gu, w_down, g_post, tm, chunk):
    bsz, s, d = x.shape
    aw = attn.shape[2]
    d_ff = w_down.shape[0]
    row_spec = lambda w: pl.BlockSpec((1, tm, w), lambda b, t: (b, t, 0))
    full = lambda shape: pl.BlockSpec(shape, lambda b, t: (0,) * len(shape),
                                      pipeline_mode=pl.Buffered(1))
    return pl.pallas_call(
        functools.partial(_mix_ffn_kernel, chunk=chunk),
        grid=(bsz, s // tm),
        in_specs=[
            row_spec(aw), row_spec(sgu.shape[2]), row_spec(d),
            pl.BlockSpec((1, N_MOD, d), lambda b, t: (b, 0, 0)),
            full((1, aw)), full(w_out.shape), full((1, d)),
            full((1, d)), full(w_gu.shape), full(w_down.shape), full((1, d)),
        ],
        out_specs=row_spec(d),
        out_shape=jax.ShapeDtypeStruct((bsz, s, d), F32),
        scratch_shapes=[pltpu.VMEM((tm, d_ff), BF16)],
        compiler_params=pltpu.CompilerParams(
            dimension_semantics=("parallel", "parallel"), vmem_limit_bytes=VMEM_LIMIT_BYTES),
        name="mix_ffn",
    )(attn, sgu, x, mod, g_attn, w_out, g_post_mix, g_pre, w_gu, w_down, g_post)


def kernel(x, c, w_ada, b_ada, g_pre_mix, g_post_mix, w_in, g_sgu_norm, w_sgu, b_sgu,
           g_attn_out, g_sgu_out, w_out, g_pre_ffn, g_post_ffn, w_gate_up, w_down):
    bsz, s, d = x.shape
    depth = w_ada.shape[0]
    d_ff = w_down.shape[1]
    assert s % MOBA_BLOCK == 0 and s // MOBA_BLOCK <= HEAD_DIM
    assert w_in.shape[2] == 3 * ATTN_WIDTH + 2 * N_SGU_GROUPS * LANES
    tm_in, tm_ffn = 512, 512
    ffn_chunk = 256
    moba_chunk = 2 * MOBA_BLOCK
    assert d_ff % ffn_chunk == 0
    row = lambda a: a.reshape(1, -1)
    for l in range(depth):
        mod = _ada(c, w_ada[l], b_ada[l]).reshape(bsz, N_MOD, d)
        q, k, v, sgu = _in_proj(
            x, mod, row(g_pre_mix[l]), w_in[l].astype(BF16), row(g_sgu_norm[l]), w_sgu[l],
            b_sgu[l].T, row(g_sgu_out[l]), tm_in)
        attn = _moba(q, k, v, moba_chunk)
        x = _mix_ffn(attn, sgu, x, mod, row(g_attn_out[l]), w_out[l].astype(BF16),
                     row(g_post_mix[l]), row(g_pre_ffn[l]), w_gate_up[l].astype(BF16),
                     w_down[l].astype(BF16), row(g_post_ffn[l]), tm_ffn, ffn_chunk)
    return x
```

```python
import functools
import math

import jax
import jax.numpy as jnp
from jax import lax
from jax.experimental import pallas as pl
from jax.experimental.pallas import tpu as pltpu

N_HEADS = 8
HEAD_DIM = 64
ATTN_WIDTH = N_HEADS * HEAD_DIM
N_SGU_GROUPS = 4
SGU_CHUNK = 128
MOBA_BLOCK = 256
MOBA_TOPK = 3
N_MOD = 6
EPS = 1e-6
NEG = -1e30

LANES = 128
VMEM_LIMIT_BYTES = 56 * 1024 * 1024
Q_SCALE = HEAD_DIM ** -0.5 * math.log2(math.e)

BF16 = jnp.bfloat16
F32 = jnp.float32


def _rms(x, g):
    return x * lax.rsqrt(jnp.mean(x * x, axis=-1, keepdims=True) + EPS) * g


def _dot(a, b):
    return jnp.dot(a, b, preferred_element_type=F32)


def _dot_nt(a, b):
    return lax.dot_general(a, b, (((1,), (1,)), ((), ())), preferred_element_type=F32)


def _ada_kernel(c_ref, w_ref, b_ref, o_ref):
    c = c_ref[...]
    c_act = (c * jax.nn.sigmoid(c)).astype(BF16)
    o_ref[...] = _dot(c_act, w_ref[...].astype(BF16)) + b_ref[...]


def _ada(c, w_ada, b_ada):
    bsz, d = c.shape
    n = w_ada.shape[1]
    tn = d
    return pl.pallas_call(
        _ada_kernel,
        grid=(n // tn,),
        in_specs=[
            pl.BlockSpec((bsz, d), lambda j: (0, 0)),
            pl.BlockSpec((d, tn), lambda j: (0, j)),
            pl.BlockSpec((1, tn), lambda j: (0, j)),
        ],
        out_specs=pl.BlockSpec((bsz, tn), lambda j: (0, j)),
        out_shape=jax.ShapeDtypeStruct((bsz, n), F32),
        compiler_params=pltpu.CompilerParams(
            dimension_semantics=("arbitrary",), vmem_limit_bytes=VMEM_LIMIT_BYTES),
        name="ada",
    )(c, w_ada, b_ada.reshape(1, n))


def _in_proj_kernel(x_ref, mod_ref, g_pre_ref, w_in_ref, g_ln_ref, w_sgu_ref, b_sgu_ref,
                    g_sgu_out_ref, q_ref, k_ref, v_ref, sgu_ref, sgu_scr, *, sub):
    tm = x_ref.shape[1]
    aw = ATTN_WIDTH
    gd = LANES
    n_chunks = sub // SGU_CHUNK
    shift = mod_ref[0, 0:1, :]
    scale = mod_ref[0, 1:2, :]
    row = lax.broadcasted_iota(jnp.int32, (SGU_CHUNK, SGU_CHUNK), 0)
    col = lax.broadcasted_iota(jnp.int32, (SGU_CHUNK, SGU_CHUNK), 1)
    w_mix = [jnp.where(col <= row, w_sgu_ref[g], 0.0).astype(BF16) for g in range(N_SGU_GROUPS)]

    subs = [slice(r0, r0 + sub) for r0 in range(0, tm, sub)]
    hs = [(_rms(x_ref[0, rows, :], g_pre_ref[...]) * (1.0 + scale) + shift).astype(BF16)
          for rows in subs]
    projs = [_dot(h, w_in_ref[...]) for h in hs]
    for rows, proj in zip(subs, projs):
        q_ref[0, rows, :] = (proj[:, 0:aw] * Q_SCALE).astype(BF16)
        k_ref[0, rows, :] = proj[:, aw:2 * aw].astype(BF16)
        v_ref[0, rows, :] = proj[:, 2 * aw:3 * aw].astype(BF16)

        u = jax.nn.gelu(proj[:, 3 * aw:3 * aw + N_SGU_GROUPS * gd])
        vg = jax.nn.gelu(proj[:, 3 * aw + N_SGU_GROUPS * gd:])
        for g in range(N_SGU_GROUPS):
            lo = g * gd
            vgg = vg[:, lo:lo + gd]
            d = vgg - jnp.mean(vgg, axis=-1, keepdims=True)
            yn = (d * lax.rsqrt(jnp.mean(d * d, axis=-1, keepdims=True) + EPS)
                  * g_ln_ref[:, lo:lo + gd]).astype(BF16)
            rhs = jnp.concatenate(
                [yn[n * SGU_CHUNK:(n + 1) * SGU_CHUNK, :] for n in range(n_chunks)], axis=1)
            mixed = _dot(w_mix[g], rhs) + b_sgu_ref[:, g:g + 1]
            for n in range(n_chunks):
                r0 = rows.start + n * SGU_CHUNK
                sgu_scr[r0:r0 + SGU_CHUNK, lo:lo + gd] = (
                    u[n * SGU_CHUNK:(n + 1) * SGU_CHUNK, lo:lo + gd]
                    * mixed[:, n * gd:(n + 1) * gd])
        sgu_ref[0, rows, :] = _rms(sgu_scr[rows, :], g_sgu_out_ref[...]).astype(BF16)


def _in_proj(x, mod, g_pre, w_in, g_ln, w_sgu, b_sgu_t, g_sgu_out, tm, sub):
    bsz, s, d = x.shape
    in_w = w_in.shape[1]
    aw = ATTN_WIDTH
    sw = in_w - 3 * aw
    sgu_w = sw // 2
    row_spec = lambda w: pl.BlockSpec((1, tm, w), lambda b, t: (b, t, 0))
    full = lambda shape: pl.BlockSpec(shape, lambda b, t: (0,) * len(shape),
                                      pipeline_mode=pl.Buffered(1))
    out_sds = jax.ShapeDtypeStruct((bsz, s, aw), BF16)
    return pl.pallas_call(
        functools.partial(_in_proj_kernel, sub=sub),
        grid=(bsz, s // tm),
        in_specs=[
            row_spec(d),
            pl.BlockSpec((1, N_MOD, d), lambda b, t: (b, 0, 0)),
            full((1, d)),
            full((d, in_w)),
            full((1, sgu_w)),
            full((N_SGU_GROUPS, SGU_CHUNK, SGU_CHUNK)),
            full((SGU_CHUNK, N_SGU_GROUPS)),
            full((1, sgu_w)),
        ],
        out_specs=[row_spec(aw), row_spec(aw), row_spec(aw), row_spec(sgu_w)],
        out_shape=[out_sds, out_sds, out_sds, jax.ShapeDtypeStruct((bsz, s, sgu_w), BF16)],
        scratch_shapes=[pltpu.VMEM((tm, sgu_w), F32)],
        compiler_params=pltpu.CompilerParams(
            dimension_semantics=("parallel", "parallel"), vmem_limit_bytes=VMEM_LIMIT_BYTES),
        name="in_proj",
    )(x, mod, g_pre, w_in, g_ln, w_sgu, b_sgu_t, g_sgu_out)


def _topk_bias(gate, n_past, base):
    t = gate.shape[0]
    rows = -(-n_past // 8) * 8
    g = gate.T[base:base + rows, :]
    nrow = lax.broadcasted_iota(jnp.int32, (rows, t), 0)
    beaten = jnp.zeros((rows, t), F32)
    for other in range(n_past):
        b = g[other:other + 1, :]
        beaten = beaten + jnp.where(b > g, 1.0, 0.0)
        beaten = beaten + jnp.where(jnp.logical_and(b == g, nrow > other), 1.0, 0.0)
    bias_t = jnp.where(nrow < n_past, jnp.where(beaten < MOBA_TOPK, 0.0, NEG), 0.0)
    pieces = [bias_t]
    if base:
        pieces.insert(0, jnp.zeros((base, t), F32))
    if base + rows < LANES:
        pieces.append(jnp.zeros((LANES - base - rows, t), F32))
    return jnp.concatenate(pieces, axis=0).T


def _moba_kernel(q_ref, k_ref, v_ref, o_ref, q_ext, k_ext, vt_ext, *, chunk):
    seq = k_ref.shape[1]
    blk = MOBA_BLOCK
    nb = seq // blk
    half = HEAD_DIM
    lane = lax.broadcasted_iota(jnp.int32, (blk, LANES), 1)
    in_head = (lane < half, lane >= half)
    bias_base = (half, 0)

    trow = lax.broadcasted_iota(jnp.int32, (LANES, blk), 0)
    in_head_t = (trow < half, trow >= half)

    for j in range(nb):
        rows = slice(j * blk, (j + 1) * blk)
        kb = k_ref[0, rows, :].astype(F32)
        vb_t = v_ref[0, rows, :].astype(F32).T
        for hd in range(2):
            onehot = jnp.where(lane == bias_base[hd] + j, 1.0, 0.0)
            k_ext[hd, rows, :] = jnp.where(in_head[hd], kb, onehot).astype(BF16)
            vt_ext[hd, :, rows] = jnp.where(in_head_t[hd], vb_t, 1.0).astype(BF16)

    prow = lax.broadcasted_iota(jnp.int32, (LANES, seq), 0)
    pcol = lax.broadcasted_iota(jnp.int32, (LANES, seq), 1)
    pr = jnp.bitwise_and(prow, half - 1)
    pblk = lax.shift_right_logical(pcol, blk.bit_length() - 1)
    psel = jnp.where(jnp.logical_and(pr == pblk, pr < nb), 1.0 / blk, 0.0).astype(BF16)
    km_full = _dot(psel, k_ref[0])
    krow = lax.broadcasted_iota(jnp.int32, (LANES, LANES), 0)
    klane = lax.broadcasted_iota(jnp.int32, (LANES, LANES), 1)
    km = (
        jnp.where(jnp.logical_and(krow >= half, klane < half), km_full, 0.0).astype(BF16),
        jnp.where(jnp.logical_and(krow < half, klane >= half), km_full, 0.0).astype(BF16),
    )

    key_pos = lax.broadcasted_iota(jnp.int32, (blk, blk), 0)
    q_pos = lax.broadcasted_iota(jnp.int32, (blk, blk), 1)
    causal = key_pos <= q_pos

    for i in range(nb):
        rows = slice(i * blk, (i + 1) * blk)
        qp = q_ref[0, rows, :].astype(F32)
        for hd in range(2):
            q0 = jnp.where(in_head[hd], qp, 0.0).astype(BF16)
            if i == 0:
                q_ext[hd, rows, :] = q0
                continue
            bias = _topk_bias(_dot_nt(q0, km[hd]), i, bias_base[hd])
            q_ext[hd, rows, :] = jnp.where(in_head[hd], qp, bias).astype(BF16)

    items = [(i, max(c1 - chunk, 0), c1)
             for i in range(nb) for c1 in range((i + 1) * blk, 0, -chunk)]

    def scores(item):
        i, c0, c1 = item
        return [_dot_nt(k_ext[hd, c0:c1, :], q_ext[hd, i * blk:(i + 1) * blk, :])
                for hd in range(2)]

    s_next = scores(items[0])
    m = [None, None]
    acc = [None, None]
    for n, (i, c0, c1) in enumerate(items):
        s_cur = s_next
        if n + 1 < len(items):
            s_next = scores(items[n + 1])
        cw = c1 - c0
        for hd in range(2):
            s = s_cur[hd]
            if c1 == (i + 1) * blk:
                own = jnp.where(causal, s[cw - blk:, :], NEG)
                s = own if cw == blk else jnp.concatenate([s[:cw - blk, :], own], axis=0)
            m_chunk = jnp.max(jnp.max(s.reshape(cw // 8, 8, blk), axis=0), axis=0, keepdims=True)
            m_new = m_chunk if m[hd] is None else jnp.maximum(m[hd], m_chunk)
            pv = _dot(vt_ext[hd, :, c0:c1], jnp.exp2(s - m_new).astype(BF16))
            acc[hd] = pv if acc[hd] is None else acc[hd] * jnp.exp2(m[hd] - m_new) + pv
            m[hd] = m_new
        if c0 == 0:
            num = jnp.where(in_head_t[0], acc[0], acc[1])
            inv = jnp.where(in_head_t[0], 1.0 / acc[0][half:half + 1, :], 1.0 / acc[1][0:1, :])
            o_ref[0, i * blk:(i + 1) * blk, :] = (num * inv).T.astype(o_ref.dtype)
            m = [None, None]
            acc = [None, None]


def _moba(q, k, v, chunk):
    bsz, s, aw = q.shape
    spec = pl.BlockSpec((1, s, LANES), lambda b, p: (b, 0, p))
    return pl.pallas_call(
        functools.partial(_moba_kernel, chunk=chunk),
        grid=(bsz, aw // LANES),
        in_specs=[spec, spec, spec],
        out_specs=spec,
        out_shape=jax.ShapeDtypeStruct((bsz, s, aw), BF16),
        scratch_shapes=[
            pltpu.VMEM((2, s, LANES), BF16),
            pltpu.VMEM((2, s, LANES), BF16),
            pltpu.VMEM((2, LANES, s), BF16),
        ],
        compiler_params=pltpu.CompilerParams(
            dimension_semantics=("parallel", "parallel"), vmem_limit_bytes=VMEM_LIMIT_BYTES),
        name="moba",
    )(q, k, v)


def _mix_ffn_kernel(attn_ref, sgu_ref, x_ref, mod_ref, g_attn_ref, w_out_ref, g_post_mix_ref,
                    g_pre_ref, w_gu_ref, w_down_ref, g_post_ref, o_ref, x_scr, act_scr, *,
                    sub, chunk):
    tm = x_ref.shape[1]
    aw = attn_ref.shape[2]
    d_ff = w_down_ref.shape[0]
    gate_m = mod_ref[0, 2:3, :]
    shift = mod_ref[0, 3:4, :]
    scale = mod_ref[0, 4:5, :]
    gate_f = mod_ref[0, 5:6, :]

    subs = [slice(r0, r0 + sub) for r0 in range(0, tm, sub)]
    hs = []
    for rows in subs:
        a = _rms(attn_ref[0, rows, :].astype(F32), g_attn_ref[...]).astype(BF16)
        mix = _dot(a, w_out_ref[0:aw, :]) + _dot(sgu_ref[0, rows, :], w_out_ref[aw:, :])
        x = x_ref[0, rows, :] + gate_m * _rms(mix, g_post_mix_ref[...])
        x_scr[rows, :] = x
        hs.append((_rms(x, g_pre_ref[...]) * (1.0 + scale) + shift).astype(BF16))
    for rows, h in zip(subs, hs):
        for c0 in range(0, d_ff, chunk):
            g = _dot(h, w_gu_ref[:, c0:c0 + chunk])
            up = _dot(h, w_gu_ref[:, d_ff + c0:d_ff + c0 + chunk])
            act_scr[rows, c0:c0 + chunk] = (g * jax.nn.sigmoid(g) * up).astype(BF16)
    for rows in subs:
        y = _dot(act_scr[rows, :], w_down_ref[...])
        o_ref[0, rows, :] = x_scr[rows, :] + gate_f * _rms(y, g_post_ref[...])


def _mix_ffn(attn, sgu, x, mod, g_attn, w_out, g_post_mix, g_pre, w_gu, w_down, g_post, tm, sub,
             chunk):
    bsz, s, d = x.shape
    aw = attn.shape[2]
    d_ff = w_down.shape[0]
    row_spec = lambda w: pl.BlockSpec((1, tm, w), lambda b, t: (b, t, 0))
    full = lambda shape: pl.BlockSpec(shape, lambda b, t: (0,) * len(shape),
                                      pipeline_mode=pl.Buffered(1))
    return pl.pallas_call(
        functools.partial(_mix_ffn_kernel, sub=sub, chunk=chunk),
        grid=(bsz, s // tm),
        in_specs=[
            row_spec(aw), row_spec(sgu.shape[2]), row_spec(d),
            pl.BlockSpec((1, N_MOD, d), lambda b, t: (b, 0, 0)),
            full((1, aw)), full(w_out.shape), full((1, d)),
            full((1, d)), full(w_gu.shape), full(w_down.shape), full((1, d)),
        ],
        out_specs=row_spec(d),
        out_shape=jax.ShapeDtypeStruct((bsz, s, d), F32),
        scratch_shapes=[pltpu.VMEM((tm, d), F32), pltpu.VMEM((tm, d_ff), BF16)],
        compiler_params=pltpu.CompilerParams(
            dimension_semantics=("parallel", "parallel"), vmem_limit_bytes=VMEM_LIMIT_BYTES),
        name="mix_ffn",
    )(attn, sgu, x, mod, g_attn, w_out, g_post_mix, g_pre, w_gu, w_down, g_post)


def kernel(x, c, w_ada, b_ada, g_pre_mix, g_post_mix, w_in, g_sgu_norm, w_sgu, b_sgu,
           g_attn_out, g_sgu_out, w_out, g_pre_ffn, g_post_ffn, w_gate_up, w_down):
    bsz, s, d = x.shape
    depth = w_ada.shape[0]
    d_ff = w_down.shape[1]
    assert s % MOBA_BLOCK == 0 and s // MOBA_BLOCK <= HEAD_DIM
    assert w_in.shape[2] == 3 * ATTN_WIDTH + 2 * N_SGU_GROUPS * LANES
    tm_in, sub_in, tm_ffn, sub_ffn = 1024, 256, 1024, 256
    ffn_chunk = 256
    moba_chunk = 2 * MOBA_BLOCK
    assert d_ff % ffn_chunk == 0
    row = lambda a: a.reshape(1, -1)
    for l in range(depth):
        mod = _ada(c, w_ada[l], b_ada[l]).reshape(bsz, N_MOD, d)
        q, k, v, sgu = _in_proj(
            x, mod, row(g_pre_mix[l]), w_in[l].astype(BF16), row(g_sgu_norm[l]), w_sgu[l],
            b_sgu[l].T, row(g_sgu_out[l]), tm_in, sub_in)
        attn = _moba(q, k, v, moba_chunk)
        x = _mix_ffn(attn, sgu, x, mod, row(g_attn_out[l]), w_out[l].astype(BF16),
                     row(g_post_mix[l]), row(g_pre_ffn[l]), w_gate_up[l].astype(BF16),
                     w_down[l].astype(BF16), row(g_post_ffn[l]), tm_ffn, sub_ffn,
                     ffn_chunk)
    return x
```

```python
import functools
import math

import jax
import jax.numpy as jnp
from jax import lax
from jax.experimental import pallas as pl
from jax.experimental.pallas import tpu as pltpu

N_HEADS = 8
HEAD_DIM = 64
ATTN_WIDTH = N_HEADS * HEAD_DIM
N_SGU_GROUPS = 4
SGU_CHUNK = 128
MOBA_BLOCK = 256
MOBA_TOPK = 3
N_MOD = 6
EPS = 1e-6
NEG = -1e30

LANES = 128
VMEM_LIMIT_BYTES = 56 * 1024 * 1024
Q_SCALE = HEAD_DIM ** -0.5 * math.log2(math.e)

BF16 = jnp.bfloat16
F32 = jnp.float32


def _rms(x, g):
    return x * lax.rsqrt(jnp.mean(x * x, axis=-1, keepdims=True) + EPS) * g


def _dot(a, b):
    return jnp.dot(a, b, preferred_element_type=F32)


def _dot_nt(a, b):
    return lax.dot_general(a, b, (((1,), (1,)), ((), ())), preferred_element_type=F32)


def _ada_kernel(c_ref, w_ref, b_ref, o_ref):
    c = c_ref[...]
    c_act = (c * jax.nn.sigmoid(c)).astype(BF16)
    o_ref[...] = _dot(c_act, w_ref[...].astype(BF16)) + b_ref[...]


def _ada(c, w_ada, b_ada):
    bsz, d = c.shape
    n = w_ada.shape[1]
    tn = d
    return pl.pallas_call(
        _ada_kernel,
        grid=(n // tn,),
        in_specs=[
            pl.BlockSpec((bsz, d), lambda j: (0, 0)),
            pl.BlockSpec((d, tn), lambda j: (0, j)),
            pl.BlockSpec((1, tn), lambda j: (0, j)),
        ],
        out_specs=pl.BlockSpec((bsz, tn), lambda j: (0, j)),
        out_shape=jax.ShapeDtypeStruct((bsz, n), F32),
        compiler_params=pltpu.CompilerParams(
            dimension_semantics=("arbitrary",), vmem_limit_bytes=VMEM_LIMIT_BYTES),
        name="ada",
    )(c, w_ada, b_ada.reshape(1, n))


def _in_proj_kernel(x_ref, mod_ref, g_pre_ref, w_in_ref, g_ln_ref, w_sgu_ref, b_sgu_ref,
                    g_sgu_out_ref, q_ref, k_ref, v_ref, sgu_ref, sgu_scr, *, sub):
    tm = x_ref.shape[1]
    aw = ATTN_WIDTH
    gd = LANES
    n_chunks = sub // SGU_CHUNK
    shift = mod_ref[0, 0:1, :]
    scale = mod_ref[0, 1:2, :]
    row = lax.broadcasted_iota(jnp.int32, (SGU_CHUNK, SGU_CHUNK), 0)
    col = lax.broadcasted_iota(jnp.int32, (SGU_CHUNK, SGU_CHUNK), 1)
    w_mix = [jnp.where(col <= row, w_sgu_ref[g], 0.0).astype(BF16) for g in range(N_SGU_GROUPS)]

    subs = [slice(r0, r0 + sub) for r0 in range(0, tm, sub)]
    hs = [(_rms(x_ref[0, rows, :], g_pre_ref[...]) * (1.0 + scale) + shift).astype(BF16)
          for rows in subs]
    projs = [_dot(h, w_in_ref[...]) for h in hs]
    for rows, proj in zip(subs, projs):
        for p in range(aw // LANES):
            lanes = slice(p * LANES, (p + 1) * LANES)
            q_ref[0, p, rows, :] = (proj[:, 0:aw][:, lanes] * Q_SCALE).astype(BF16)
            k_ref[0, p, rows, :] = proj[:, aw:2 * aw][:, lanes].astype(BF16)
            v_ref[0, p, rows, :] = proj[:, 2 * aw:3 * aw][:, lanes].astype(BF16)

        u = jax.nn.gelu(proj[:, 3 * aw:3 * aw + N_SGU_GROUPS * gd])
        vg = jax.nn.gelu(proj[:, 3 * aw + N_SGU_GROUPS * gd:])
        for g in range(N_SGU_GROUPS):
            lo = g * gd
            vgg = vg[:, lo:lo + gd]
            d = vgg - jnp.mean(vgg, axis=-1, keepdims=True)
            yn = (d * lax.rsqrt(jnp.mean(d * d, axis=-1, keepdims=True) + EPS)
                  * g_ln_ref[:, lo:lo + gd]).astype(BF16)
            rhs = jnp.concatenate(
                [yn[n * SGU_CHUNK:(n + 1) * SGU_CHUNK, :] for n in range(n_chunks)], axis=1)
            mixed = _dot(w_mix[g], rhs) + b_sgu_ref[:, g:g + 1]
            for n in range(n_chunks):
                r0 = rows.start + n * SGU_CHUNK
                sgu_scr[r0:r0 + SGU_CHUNK, lo:lo + gd] = (
                    u[n * SGU_CHUNK:(n + 1) * SGU_CHUNK, lo:lo + gd]
                    * mixed[:, n * gd:(n + 1) * gd])
        sgu_ref[0, rows, :] = _rms(sgu_scr[rows, :], g_sgu_out_ref[...]).astype(BF16)


def _in_proj(x, mod, g_pre, w_in, g_ln, w_sgu, b_sgu_t, g_sgu_out, tm, sub):
    bsz, s, d = x.shape
    in_w = w_in.shape[1]
    aw = ATTN_WIDTH
    sw = in_w - 3 * aw
    sgu_w = sw // 2
    row_spec = lambda w: pl.BlockSpec((1, tm, w), lambda b, t: (b, t, 0))
    full = lambda shape: pl.BlockSpec(shape, lambda b, t: (0,) * len(shape),
                                      pipeline_mode=pl.Buffered(1))
    n_pairs = aw // LANES
    pair_sds = jax.ShapeDtypeStruct((bsz, n_pairs, s, LANES), BF16)
    pair_spec = pl.BlockSpec((1, n_pairs, tm, LANES), lambda b, t: (b, 0, t, 0))
    return pl.pallas_call(
        functools.partial(_in_proj_kernel, sub=sub),
        grid=(bsz, s // tm),
        in_specs=[
            row_spec(d),
            pl.BlockSpec((1, N_MOD, d), lambda b, t: (b, 0, 0)),
            full((1, d)),
            full((d, in_w)),
            full((1, sgu_w)),
            full((N_SGU_GROUPS, SGU_CHUNK, SGU_CHUNK)),
            full((SGU_CHUNK, N_SGU_GROUPS)),
            full((1, sgu_w)),
        ],
        out_specs=[pair_spec, pair_spec, pair_spec, row_spec(sgu_w)],
        out_shape=[pair_sds, pair_sds, pair_sds, jax.ShapeDtypeStruct((bsz, s, sgu_w), BF16)],
        scratch_shapes=[pltpu.VMEM((tm, sgu_w), F32)],
        compiler_params=pltpu.CompilerParams(
            dimension_semantics=("parallel", "parallel"), vmem_limit_bytes=VMEM_LIMIT_BYTES),
        name="in_proj",
    )(x, mod, g_pre, w_in, g_ln, w_sgu, b_sgu_t, g_sgu_out)


def _topk_bias(gate, n_past, base):
    t = gate.shape[0]
    rows = -(-n_past // 8) * 8
    g = gate.T[base:base + rows, :]
    nrow = lax.broadcasted_iota(jnp.int32, (rows, t), 0)
    beaten = jnp.zeros((rows, t), F32)
    for other in range(n_past):
        b = g[other:other + 1, :]
        beaten = beaten + jnp.where(b > g, 1.0, 0.0)
        beaten = beaten + jnp.where(jnp.logical_and(b == g, nrow > other), 1.0, 0.0)
    bias_t = jnp.where(nrow < n_past, jnp.where(beaten < MOBA_TOPK, 0.0, NEG), 0.0)
    pieces = [bias_t]
    if base:
        pieces.insert(0, jnp.zeros((base, t), F32))
    if base + rows < LANES:
        pieces.append(jnp.zeros((LANES - base - rows, t), F32))
    return jnp.concatenate(pieces, axis=0).T


def _moba_kernel(q_ref, k_ref, v_ref, o_ref, q_ext, k_ext, vt_ext, *, chunk):
    seq = k_ref.shape[2]
    blk = MOBA_BLOCK
    nb = seq // blk
    half = HEAD_DIM
    lane = lax.broadcasted_iota(jnp.int32, (blk, LANES), 1)
    in_head = (lane < half, lane >= half)
    bias_base = (half, 0)

    trow = lax.broadcasted_iota(jnp.int32, (LANES, blk), 0)
    in_head_t = (trow < half, trow >= half)

    for j in range(nb):
        rows = slice(j * blk, (j + 1) * blk)
        kb = k_ref[0, 0, rows, :].astype(F32)
        vb_t = v_ref[0, 0, rows, :].astype(F32).T
        for hd in range(2):
            onehot = jnp.where(lane == bias_base[hd] + j, 1.0, 0.0)
            k_ext[hd, rows, :] = jnp.where(in_head[hd], kb, onehot).astype(BF16)
            vt_ext[hd, :, rows] = jnp.where(in_head_t[hd], vb_t, 1.0).astype(BF16)

    prow = lax.broadcasted_iota(jnp.int32, (LANES, seq), 0)
    pcol = lax.broadcasted_iota(jnp.int32, (LANES, seq), 1)
    pr = jnp.bitwise_and(prow, half - 1)
    pblk = lax.shift_right_logical(pcol, blk.bit_length() - 1)
    psel = jnp.where(jnp.logical_and(pr == pblk, pr < nb), 1.0 / blk, 0.0).astype(BF16)
    km_full = _dot(psel, k_ref[0, 0])
    krow = lax.broadcasted_iota(jnp.int32, (LANES, LANES), 0)
    klane = lax.broadcasted_iota(jnp.int32, (LANES, LANES), 1)
    km = (
        jnp.where(jnp.logical_and(krow >= half, klane < half), km_full, 0.0).astype(BF16),
        jnp.where(jnp.logical_and(krow < half, klane >= half), km_full, 0.0).astype(BF16),
    )

    key_pos = lax.broadcasted_iota(jnp.int32, (blk, blk), 0)
    q_pos = lax.broadcasted_iota(jnp.int32, (blk, blk), 1)
    causal = key_pos <= q_pos

    for i in range(nb):
        rows = slice(i * blk, (i + 1) * blk)
        qp = q_ref[0, 0, rows, :].astype(F32)
        for hd in range(2):
            q0 = jnp.where(in_head[hd], qp, 0.0).astype(BF16)
            if i == 0:
                q_ext[hd, rows, :] = q0
                continue
            bias = _topk_bias(_dot_nt(q0, km[hd]), i, bias_base[hd])
            q_ext[hd, rows, :] = jnp.where(in_head[hd], qp, bias).astype(BF16)

    items = [(i, max(c1 - chunk, 0), c1)
             for i in range(nb) for c1 in range((i + 1) * blk, 0, -chunk)]

    def scores(item):
        i, c0, c1 = item
        return [_dot_nt(k_ext[hd, c0:c1, :], q_ext[hd, i * blk:(i + 1) * blk, :])
                for hd in range(2)]

    s_next = scores(items[0])
    m = [None, None]
    acc = [None, None]
    for n, (i, c0, c1) in enumerate(items):
        s_cur = s_next
        if n + 1 < len(items):
            s_next = scores(items[n + 1])
        cw = c1 - c0
        for hd in range(2):
            s = s_cur[hd]
            if c1 == (i + 1) * blk:
                own = jnp.where(causal, s[cw - blk:, :], NEG)
                s = own if cw == blk else jnp.concatenate([s[:cw - blk, :], own], axis=0)
            m_chunk = jnp.max(jnp.max(s.reshape(cw // 8, 8, blk), axis=0), axis=0, keepdims=True)
            m_new = m_chunk if m[hd] is None else jnp.maximum(m[hd], m_chunk)
            pv = _dot(vt_ext[hd, :, c0:c1], jnp.exp2(s - m_new).astype(BF16))
            acc[hd] = pv if acc[hd] is None else acc[hd] * jnp.exp2(m[hd] - m_new) + pv
            m[hd] = m_new
        if c0 == 0:
            num = jnp.where(in_head_t[0], acc[0], acc[1])
            inv = jnp.where(in_head_t[0], 1.0 / acc[0][half:half + 1, :], 1.0 / acc[1][0:1, :])
            o_ref[0, 0, i * blk:(i + 1) * blk, :] = (num * inv).T.astype(o_ref.dtype)
            m = [None, None]
            acc = [None, None]


def _moba(q, k, v, chunk):
    bsz, n_pairs, s, _ = q.shape
    spec = pl.BlockSpec((1, 1, s, LANES), lambda b, p: (b, p, 0, 0))
    return pl.pallas_call(
        functools.partial(_moba_kernel, chunk=chunk),
        grid=(bsz, n_pairs),
        in_specs=[spec, spec, spec],
        out_specs=spec,
        out_shape=jax.ShapeDtypeStruct(q.shape, BF16),
        scratch_shapes=[
            pltpu.VMEM((2, s, LANES), BF16),
            pltpu.VMEM((2, s, LANES), BF16),
            pltpu.VMEM((2, LANES, s), BF16),
        ],
        compiler_params=pltpu.CompilerParams(
            dimension_semantics=("parallel", "parallel"), vmem_limit_bytes=VMEM_LIMIT_BYTES),
        name="moba",
    )(q, k, v)


def _mix_ffn_kernel(attn_ref, sgu_ref, x_ref, mod_ref, g_attn_ref, w_out_ref, g_post_mix_ref,
                    g_pre_ref, w_gu_ref, w_down_ref, g_post_ref, o_ref, x_scr, act_scr, *,
                    sub, chunk):
    tm = x_ref.shape[1]
    aw = attn_ref.shape[1] * attn_ref.shape[3]
    d_ff = w_down_ref.shape[0]
    gate_m = mod_ref[0, 2:3, :]
    shift = mod_ref[0, 3:4, :]
    scale = mod_ref[0, 4:5, :]
    gate_f = mod_ref[0, 5:6, :]

    subs = [slice(r0, r0 + sub) for r0 in range(0, tm, sub)]
    hs = []
    for rows in subs:
        attn = jnp.concatenate(
            [attn_ref[0, p, rows, :].astype(F32) for p in range(attn_ref.shape[1])], axis=1)
        a = _rms(attn, g_attn_ref[...]).astype(BF16)
        mix = _dot(a, w_out_ref[0:aw, :]) + _dot(sgu_ref[0, rows, :], w_out_ref[aw:, :])
        x = x_ref[0, rows, :] + gate_m * _rms(mix, g_post_mix_ref[...])
        x_scr[rows, :] = x
        hs.append((_rms(x, g_pre_ref[...]) * (1.0 + scale) + shift).astype(BF16))
    for rows, h in zip(subs, hs):
        for c0 in range(0, d_ff, chunk):
            g = _dot(h, w_gu_ref[:, c0:c0 + chunk])
            up = _dot(h, w_gu_ref[:, d_ff + c0:d_ff + c0 + chunk])
            act_scr[rows, c0:c0 + chunk] = (g * jax.nn.sigmoid(g) * up).astype(BF16)
    for rows in subs:
        y = _dot(act_scr[rows, :], w_down_ref[...])
        o_ref[0, rows, :] = x_scr[rows, :] + gate_f * _rms(y, g_post_ref[...])


def _mix_ffn(attn, sgu, x, mod, g_attn, w_out, g_post_mix, g_pre, w_gu, w_down, g_post, tm, sub,
             chunk):
    bsz, s, d = x.shape
    n_pairs = attn.shape[1]
    aw = n_pairs * attn.shape[3]
    d_ff = w_down.shape[0]
    row_spec = lambda w: pl.BlockSpec((1, tm, w), lambda b, t: (b, t, 0))
    full = lambda shape: pl.BlockSpec(shape, lambda b, t: (0,) * len(shape),
                                      pipeline_mode=pl.Buffered(1))
    return pl.pallas_call(
        functools.partial(_mix_ffn_kernel, sub=sub, chunk=chunk),
        grid=(bsz, s // tm),
        in_specs=[
            pl.BlockSpec((1, n_pairs, tm, attn.shape[3]), lambda b, t: (b, 0, t, 0)),
            row_spec(sgu.shape[2]), row_spec(d),
            pl.BlockSpec((1, N_MOD, d), lambda b, t: (b, 0, 0)),
            full((1, aw)), full(w_out.shape), full((1, d)),
            full((1, d)), full(w_gu.shape), full(w_down.shape), full((1, d)),
        ],
        out_specs=row_spec(d),
        out_shape=jax.ShapeDtypeStruct((bsz, s, d), F32),
        scratch_shapes=[pltpu.VMEM((tm, d), F32), pltpu.VMEM((tm, d_ff), BF16)],
        compiler_params=pltpu.CompilerParams(
            dimension_semantics=("parallel", "parallel"), vmem_limit_bytes=VMEM_LIMIT_BYTES),
        name="mix_ffn",
    )(attn, sgu, x, mod, g_attn, w_out, g_post_mix, g_pre, w_gu, w_down, g_post)


def kernel(x, c, w_ada, b_ada, g_pre_mix, g_post_mix, w_in, g_sgu_norm, w_sgu, b_sgu,
           g_attn_out, g_sgu_out, w_out, g_pre_ffn, g_post_ffn, w_gate_up, w_down):
    bsz, s, d = x.shape
    depth = w_ada.shape[0]
    d_ff = w_down.shape[1]
    assert s % MOBA_BLOCK == 0 and s // MOBA_BLOCK <= HEAD_DIM
    assert w_in.shape[2] == 3 * ATTN_WIDTH + 2 * N_SGU_GROUPS * LANES
    tm_in, sub_in, tm_ffn, sub_ffn = 1024, 256, 1024, 256
    ffn_chunk = 256
    moba_chunk = 2 * MOBA_BLOCK
    assert d_ff % ffn_chunk == 0
    row = lambda a: a.reshape(1, -1)
    for l in range(depth):
        mod = _ada(c, w_ada[l], b_ada[l]).reshape(bsz, N_MOD, d)
        q, k, v, sgu = _in_proj(
            x, mod, row(g_pre_mix[l]), w_in[l].astype(BF16), row(g_sgu_norm[l]), w_sgu[l],
            b_sgu[l].T, row(g_sgu_out[l]), tm_in, sub_in)
        attn = _moba(q, k, v, moba_chunk)
        x = _mix_ffn(attn, sgu, x, mod, row(g_attn_out[l]), w_out[l].astype(BF16),
                     row(g_post_mix[l]), row(g_pre_ffn[l]), w_gate_up[l].astype(BF16),
                     w_down[l].astype(BF16), row(g_post_ffn[l]), tm_ffn, sub_ffn,
                     ffn_chunk)
    return x
```

```python
import functools
import math

import jax
import jax.numpy as jnp
from jax import lax
from jax.experimental import pallas as pl
from jax.experimental.pallas import tpu as pltpu

N_HEADS = 8
HEAD_DIM = 64
ATTN_WIDTH = N_HEADS * HEAD_DIM
N_SGU_GROUPS = 4
SGU_CHUNK = 128
MOBA_BLOCK = 256
MOBA_TOPK = 3
N_MOD = 6
EPS = 1e-6
NEG = -1e30

LANES = 128
DEN_ROWS = 16
VMEM_LIMIT_BYTES = 56 * 1024 * 1024
Q_SCALE = HEAD_DIM ** -0.5 * math.log2(math.e)

BF16 = jnp.bfloat16
F32 = jnp.float32


def _rms(x, g):
    return x * lax.rsqrt(jnp.mean(x * x, axis=-1, keepdims=True) + EPS) * g


def _dot(a, b):
    return jnp.dot(a, b, preferred_element_type=F32)


def _ada_kernel(c_ref, w_ref, b_ref, o_ref):
    c = c_ref[...]
    c_act = (c * jax.nn.sigmoid(c)).astype(BF16)
    o_ref[...] = _dot(c_act, w_ref[...].astype(BF16)) + b_ref[...]


def _ada(c, w_ada, b_ada):
    bsz, d = c.shape
    n = w_ada.shape[1]
    tn = d
    return pl.pallas_call(
        _ada_kernel,
        grid=(n // tn,),
        in_specs=[
            pl.BlockSpec((bsz, d), lambda j: (0, 0)),
            pl.BlockSpec((d, tn), lambda j: (0, j)),
            pl.BlockSpec((1, tn), lambda j: (0, j)),
        ],
        out_specs=pl.BlockSpec((bsz, tn), lambda j: (0, j)),
        out_shape=jax.ShapeDtypeStruct((bsz, n), F32),
        compiler_params=pltpu.CompilerParams(
            dimension_semantics=("arbitrary",), vmem_limit_bytes=VMEM_LIMIT_BYTES),
        name="ada",
    )(c, w_ada, b_ada.reshape(1, n))


def _in_proj_kernel(x_ref, mod_ref, g_pre_ref, w_in_ref, g_ln_ref, w_sgu_ref, b_sgu_ref,
                    g_sgu_out_ref, q_ref, k_ref, v_ref, sgu_ref, sgu_scr, *, sub):
    tm = x_ref.shape[1]
    aw = ATTN_WIDTH
    gd = LANES
    n_chunks = sub // SGU_CHUNK
    shift = mod_ref[0, 0:1, :]
    scale = mod_ref[0, 1:2, :]
    row = lax.broadcasted_iota(jnp.int32, (SGU_CHUNK, SGU_CHUNK), 0)
    col = lax.broadcasted_iota(jnp.int32, (SGU_CHUNK, SGU_CHUNK), 1)
    w_mix = [jnp.where(col <= row, w_sgu_ref[g], 0.0).astype(BF16) for g in range(N_SGU_GROUPS)]

    subs = [slice(r0, r0 + sub) for r0 in range(0, tm, sub)]
    hs = [(_rms(x_ref[0, rows, :], g_pre_ref[...]) * (1.0 + scale) + shift).astype(BF16)
          for rows in subs]
    projs = [_dot(h, w_in_ref[...]) for h in hs]
    for rows, proj in zip(subs, projs):
        for p in range(aw // LANES):
            lanes = slice(p * LANES, (p + 1) * LANES)
            q_ref[0, p, rows, :] = (proj[:, 0:aw][:, lanes] * Q_SCALE).astype(BF16)
            k_ref[0, p, rows, :] = proj[:, aw:2 * aw][:, lanes].astype(BF16)
            v_ref[0, p, rows, :] = proj[:, 2 * aw:3 * aw][:, lanes].astype(BF16)

        u = jax.nn.gelu(proj[:, 3 * aw:3 * aw + N_SGU_GROUPS * gd])
        vg = jax.nn.gelu(proj[:, 3 * aw + N_SGU_GROUPS * gd:])
        for g in range(N_SGU_GROUPS):
            lo = g * gd
            vgg = vg[:, lo:lo + gd]
            d = vgg - jnp.mean(vgg, axis=-1, keepdims=True)
            yn = (d * lax.rsqrt(jnp.mean(d * d, axis=-1, keepdims=True) + EPS)
                  * g_ln_ref[:, lo:lo + gd]).astype(BF16)
            rhs = jnp.concatenate(
                [yn[n * SGU_CHUNK:(n + 1) * SGU_CHUNK, :] for n in range(n_chunks)], axis=1)
            mixed = _dot(w_mix[g], rhs) + b_sgu_ref[:, g:g + 1]
            for n in range(n_chunks):
                r0 = rows.start + n * SGU_CHUNK
                sgu_scr[r0:r0 + SGU_CHUNK, lo:lo + gd] = (
                    u[n * SGU_CHUNK:(n + 1) * SGU_CHUNK, lo:lo + gd]
                    * mixed[:, n * gd:(n + 1) * gd])
        sgu_ref[0, rows, :] = _rms(sgu_scr[rows, :], g_sgu_out_ref[...]).astype(BF16)


def _in_proj(x, mod, g_pre, w_in, g_ln, w_sgu, b_sgu_t, g_sgu_out, tm, sub):
    bsz, s, d = x.shape
    in_w = w_in.shape[1]
    aw = ATTN_WIDTH
    sw = in_w - 3 * aw
    sgu_w = sw // 2
    row_spec = lambda w: pl.BlockSpec((1, tm, w), lambda b, t: (b, t, 0))
    full = lambda shape: pl.BlockSpec(shape, lambda b, t: (0,) * len(shape),
                                      pipeline_mode=pl.Buffered(1))
    n_pairs = aw // LANES
    pair_sds = jax.ShapeDtypeStruct((bsz, n_pairs, s, LANES), BF16)
    pair_spec = pl.BlockSpec((1, n_pairs, tm, LANES), lambda b, t: (b, 0, t, 0))
    return pl.pallas_call(
        functools.partial(_in_proj_kernel, sub=sub),
        grid=(bsz, s // tm),
        in_specs=[
            row_spec(d),
            pl.BlockSpec((1, N_MOD, d), lambda b, t: (b, 0, 0)),
            full((1, d)),
            full((d, in_w)),
            full((1, sgu_w)),
            full((N_SGU_GROUPS, SGU_CHUNK, SGU_CHUNK)),
            full((SGU_CHUNK, N_SGU_GROUPS)),
            full((1, sgu_w)),
        ],
        out_specs=[pair_spec, pair_spec, pair_spec, row_spec(sgu_w)],
        out_shape=[pair_sds, pair_sds, pair_sds, jax.ShapeDtypeStruct((bsz, s, sgu_w), BF16)],
        scratch_shapes=[pltpu.VMEM((tm, sgu_w), F32)],
        compiler_params=pltpu.CompilerParams(
            dimension_semantics=("parallel", "parallel"), vmem_limit_bytes=VMEM_LIMIT_BYTES),
        name="in_proj",
    )(x, mod, g_pre, w_in, g_ln, w_sgu, b_sgu_t, g_sgu_out)


def _topk_bias_t(g, n_past):
    nrow = lax.broadcasted_iota(jnp.int32, g.shape, 0)
    beaten = jnp.zeros(g.shape, F32)
    for other in range(n_past):
        b = g[other:other + 1, :]
        beaten = beaten + jnp.where(b > g, 1.0, 0.0)
        beaten = beaten + jnp.where(jnp.logical_and(b == g, nrow > other), 1.0, 0.0)
    return jnp.where(nrow < n_past, jnp.where(beaten < MOBA_TOPK, 0.0, NEG), 0.0)


def _moba_kernel(q_ref, k_ref, v_ref, o_ref, qt_ext, k_ext, vt_ext, *, chunk):
    seq = k_ref.shape[2]
    blk = MOBA_BLOCK
    nb = seq // blk
    half = HEAD_DIM
    lane = lax.broadcasted_iota(jnp.int32, (blk, LANES), 1)
    in_head = (lane < half, lane >= half)
    bias_base = (half, 0)

    trow = lax.broadcasted_iota(jnp.int32, (LANES, blk), 0)
    in_head_t = (trow < half, trow >= half)
    v_rows = (slice(0, half + DEN_ROWS), slice(half - DEN_ROWS, LANES))
    num_rows = (slice(0, half), slice(DEN_ROWS, DEN_ROWS + half))
    den_row = (half, 0)

    for j in range(nb):
        rows = slice(j * blk, (j + 1) * blk)
        kb = k_ref[0, 0, rows, :].astype(F32)
        vb_t = v_ref[0, 0, rows, :].astype(F32).T
        for hd in range(2):
            onehot = jnp.where(lane == bias_base[hd] + j, 1.0, 0.0)
            k_ext[hd, rows, :] = jnp.where(in_head[hd], kb, onehot).astype(BF16)
            vt_ext[hd, :, rows] = jnp.where(in_head_t[hd], vb_t, 1.0).astype(BF16)

    nb_rows = -(-nb // 8) * 8
    prow = lax.broadcasted_iota(jnp.int32, (nb_rows, seq), 0)
    pcol = lax.broadcasted_iota(jnp.int32, (nb_rows, seq), 1)
    pblk = lax.shift_right_logical(pcol, blk.bit_length() - 1)
    psel = jnp.where(prow == pblk, 1.0 / blk, 0.0).astype(BF16)
    km_pair = _dot(psel, k_ref[0, 0])
    klane = lax.broadcasted_iota(jnp.int32, (nb_rows, LANES), 1)
    km = (jnp.where(klane < half, km_pair, 0.0).astype(BF16),
          jnp.where(klane >= half, km_pair, 0.0).astype(BF16))

    key_pos = lax.broadcasted_iota(jnp.int32, (blk, blk), 0)
    q_pos = lax.broadcasted_iota(jnp.int32, (blk, blk), 1)
    causal = key_pos <= q_pos

    zeros_half = jnp.zeros((half, blk), F32)
    for i in range(nb):
        cols = slice(i * blk, (i + 1) * blk)
        q_t = q_ref[0, 0, cols, :].astype(F32).T
        q_own = (q_t[:half, :], q_t[half:, :])
        for hd in range(2):
            other = zeros_half
            if i > 0:
                pair = [q_own[0], zeros_half] if hd == 0 else [zeros_half, q_own[1]]
                rows = -(-i // 8) * 8
                gate_t = _dot(km[hd][:rows, :], jnp.concatenate(pair, axis=0).astype(BF16))
                other = jnp.concatenate(
                    [_topk_bias_t(gate_t, i), jnp.zeros((half - rows, blk), F32)], axis=0)
            pair = [q_own[0], other] if hd == 0 else [other, q_own[1]]
            qt_ext[hd, :, cols] = jnp.concatenate(pair, axis=0).astype(BF16)

    items = [(i, max(c1 - chunk, 0), c1)
             for i in range(nb) for c1 in range((i + 1) * blk, 0, -chunk)]

    def scores(item):
        i, c0, c1 = item
        return [_dot(k_ext[hd, c0:c1, :], qt_ext[hd, :, i * blk:(i + 1) * blk])
                for hd in range(2)]

    s_next = scores(items[0])
    m = [None, None]
    acc = [None, None]
    for n, (i, c0, c1) in enumerate(items):
        s_cur = s_next
        if n + 1 < len(items):
            s_next = scores(items[n + 1])
        cw = c1 - c0
        for hd in range(2):
            s = s_cur[hd]
            if c1 == (i + 1) * blk:
                own = jnp.where(causal, s[cw - blk:, :], NEG)
                s = own if cw == blk else jnp.concatenate([s[:cw - blk, :], own], axis=0)
            m_chunk = jnp.max(jnp.max(s.reshape(cw // 8, 8, blk), axis=0), axis=0, keepdims=True)
            m_new = m_chunk if m[hd] is None else jnp.maximum(m[hd], m_chunk)
            pv = _dot(vt_ext[hd, v_rows[hd], c0:c1], jnp.exp2(s - m_new).astype(BF16))
            acc[hd] = pv if acc[hd] is None else acc[hd] * jnp.exp2(m[hd] - m_new) + pv
            m[hd] = m_new
        if c0 == 0:
            out_t = jnp.concatenate(
                [acc[hd][num_rows[hd], :] / acc[hd][den_row[hd]:den_row[hd] + 1, :]
                 for hd in range(2)], axis=0)
            o_ref[0, 0, i * blk:(i + 1) * blk, :] = out_t.T.astype(o_ref.dtype)
            m, acc = [None, None], [None, None]


def _moba(q, k, v, chunk):
    bsz, n_pairs, s, _ = q.shape
    spec = pl.BlockSpec((1, 1, s, LANES), lambda b, p: (b, p, 0, 0))
    return pl.pallas_call(
        functools.partial(_moba_kernel, chunk=chunk),
        grid=(bsz, n_pairs),
        in_specs=[spec, spec, spec],
        out_specs=spec,
        out_shape=jax.ShapeDtypeStruct(q.shape, BF16),
        scratch_shapes=[
            pltpu.VMEM((2, LANES, s), BF16),
            pltpu.VMEM((2, s, LANES), BF16),
            pltpu.VMEM((2, LANES, s), BF16),
        ],
        compiler_params=pltpu.CompilerParams(
            dimension_semantics=("parallel", "parallel"), vmem_limit_bytes=VMEM_LIMIT_BYTES),
        name="moba",
    )(q, k, v)


def _mix_ffn_kernel(attn_ref, sgu_ref, x_ref, mod_ref, g_attn_ref, w_out_ref, g_post_mix_ref,
                    g_pre_ref, w_gu_ref, w_down_ref, g_post_ref, o_ref, x_scr, act_scr, *,
                    sub, chunk):
    tm = x_ref.shape[1]
    aw = attn_ref.shape[1] * attn_ref.shape[3]
    d_ff = w_down_ref.shape[0]
    gate_m = mod_ref[0, 2:3, :]
    shift = mod_ref[0, 3:4, :]
    scale = mod_ref[0, 4:5, :]
    gate_f = mod_ref[0, 5:6, :]

    subs = [slice(r0, r0 + sub) for r0 in range(0, tm, sub)]
    hs = []
    for rows in subs:
        attn = jnp.concatenate(
            [attn_ref[0, p, rows, :].astype(F32) for p in range(attn_ref.shape[1])], axis=1)
        a = _rms(attn, g_attn_ref[...]).astype(BF16)
        mix = _dot(a, w_out_ref[0:aw, :]) + _dot(sgu_ref[0, rows, :], w_out_ref[aw:, :])
        x = x_ref[0, rows, :] + gate_m * _rms(mix, g_post_mix_ref[...])
        x_scr[rows, :] = x
        hs.append((_rms(x, g_pre_ref[...]) * (1.0 + scale) + shift).astype(BF16))
    for rows, h in zip(subs, hs):
        for c0 in range(0, d_ff, chunk):
            g = _dot(h, w_gu_ref[:, c0:c0 + chunk])
            up = _dot(h, w_gu_ref[:, d_ff + c0:d_ff + c0 + chunk])
            act_scr[rows, c0:c0 + chunk] = (g * jax.nn.sigmoid(g) * up).astype(BF16)
    for rows in subs:
        y = _dot(act_scr[rows, :], w_down_ref[...])
        o_ref[0, rows, :] = x_scr[rows, :] + gate_f * _rms(y, g_post_ref[...])


def _mix_ffn(attn, sgu, x, mod, g_attn, w_out, g_post_mix, g_pre, w_gu, w_down, g_post, tm, sub,
             chunk):
    bsz, s, d = x.shape
    n_pairs = attn.shape[1]
    aw = n_pairs * attn.shape[3]
    d_ff = w_down.shape[0]
    row_spec = lambda w: pl.BlockSpec((1, tm, w), lambda b, t: (b, t, 0))
    full = lambda shape: pl.BlockSpec(shape, lambda b, t: (0,) * len(shape),
                                      pipeline_mode=pl.Buffered(1))
    return pl.pallas_call(
        functools.partial(_mix_ffn_kernel, sub=sub, chunk=chunk),
        grid=(bsz, s // tm),
        in_specs=[
            pl.BlockSpec((1, n_pairs, tm, attn.shape[3]), lambda b, t: (b, 0, t, 0)),
            row_spec(sgu.shape[2]), row_spec(d),
            pl.BlockSpec((1, N_MOD, d), lambda b, t: (b, 0, 0)),
            full((1, aw)), full(w_out.shape), full((1, d)),
            full((1, d)), full(w_gu.shape), full(w_down.shape), full((1, d)),
        ],
        out_specs=row_spec(d),
        out_shape=jax.ShapeDtypeStruct((bsz, s, d), F32),
        scratch_shapes=[pltpu.VMEM((tm, d), F32), pltpu.VMEM((tm, d_ff), BF16)],
        compiler_params=pltpu.CompilerParams(
            dimension_semantics=("parallel", "parallel"), vmem_limit_bytes=VMEM_LIMIT_BYTES),
        name="mix_ffn",
    )(attn, sgu, x, mod, g_attn, w_out, g_post_mix, g_pre, w_gu, w_down, g_post)


def kernel(x, c, w_ada, b_ada, g_pre_mix, g_post_mix, w_in, g_sgu_norm, w_sgu, b_sgu,
           g_attn_out, g_sgu_out, w_out, g_pre_ffn, g_post_ffn, w_gate_up, w_down):
    bsz, s, d = x.shape
    depth = w_ada.shape[0]
    d_ff = w_down.shape[1]
    assert s % MOBA_BLOCK == 0 and s // MOBA_BLOCK <= HEAD_DIM
    assert w_in.shape[2] == 3 * ATTN_WIDTH + 2 * N_SGU_GROUPS * LANES
    tm_in, sub_in, tm_ffn, sub_ffn = 1024, 256, 1024, 256
    ffn_chunk = 256
    moba_chunk = 2 * MOBA_BLOCK
    assert d_ff % ffn_chunk == 0
    row = lambda a: a.reshape(1, -1)
    for l in range(depth):
        mod = _ada(c, w_ada[l], b_ada[l]).reshape(bsz, N_MOD, d)
        q, k, v, sgu = _in_proj(
            x, mod, row(g_pre_mix[l]), w_in[l].astype(BF16), row(g_sgu_norm[l]), w_sgu[l],
            b_sgu[l].T, row(g_sgu_out[l]), tm_in, sub_in)
        attn = _moba(q, k, v, moba_chunk)
        x = _mix_ffn(attn, sgu, x, mod, row(g_attn_out[l]), w_out[l].astype(BF16),
                     row(g_post_mix[l]), row(g_pre_ffn[l]), w_gate_up[l].astype(BF16),
                     w_down[l].astype(BF16), row(g_post_ffn[l]), tm_ffn, sub_ffn,
                     ffn_chunk)
    return x
```

```python
import functools
import math

import jax
import jax.numpy as jnp
from jax import lax
from jax.experimental import pallas as pl
from jax.experimental.pallas import tpu as pltpu

N_HEADS = 8
HEAD_DIM = 64
ATTN_WIDTH = N_HEADS * HEAD_DIM
N_SGU_GROUPS = 4
SGU_CHUNK = 128
MOBA_BLOCK = 256
MOBA_TOPK = 3
N_MOD = 6
EPS = 1e-6
NEG = -1e30

LANES = 128
DEN_ROWS = 16
VMEM_LIMIT_BYTES = 56 * 1024 * 1024
Q_SCALE = HEAD_DIM ** -0.5 * math.log2(math.e)

BF16 = jnp.bfloat16
F32 = jnp.float32


def _rms(x, g):
    return x * lax.rsqrt(jnp.mean(x * x, axis=-1, keepdims=True) + EPS) * g


def _dot(a, b):
    return jnp.dot(a, b, preferred_element_type=F32)


def _ada_kernel(c_ref, w_ref, b_ref, o_ref):
    c = c_ref[...]
    c_act = (c * jax.nn.sigmoid(c)).astype(BF16)
    o_ref[...] = _dot(c_act, w_ref[...].astype(BF16)) + b_ref[...]


def _ada(c, w_ada, b_ada):
    bsz, d = c.shape
    n = w_ada.shape[1]
    tn = d
    return pl.pallas_call(
        _ada_kernel,
        grid=(n // tn,),
        in_specs=[
            pl.BlockSpec((bsz, d), lambda j: (0, 0)),
            pl.BlockSpec((d, tn), lambda j: (0, j)),
            pl.BlockSpec((1, tn), lambda j: (0, j)),
        ],
        out_specs=pl.BlockSpec((bsz, tn), lambda j: (0, j)),
        out_shape=jax.ShapeDtypeStruct((bsz, n), F32),
        compiler_params=pltpu.CompilerParams(
            dimension_semantics=("arbitrary",), vmem_limit_bytes=VMEM_LIMIT_BYTES),
        name="ada",
    )(c, w_ada, b_ada.reshape(1, n))


def _in_proj_kernel(x_ref, mod_ref, g_pre_ref, w_in_ref, g_ln_ref, w_sgu_ref, b_sgu_ref,
                    g_sgu_out_ref, q_ref, k_ref, v_ref, sgu_ref, sgu_scr, *, sub):
    tm = x_ref.shape[1]
    aw = ATTN_WIDTH
    gd = LANES
    n_chunks = sub // SGU_CHUNK
    shift = mod_ref[0, 0:1, :]
    scale = mod_ref[0, 1:2, :]
    row = lax.broadcasted_iota(jnp.int32, (SGU_CHUNK, SGU_CHUNK), 0)
    col = lax.broadcasted_iota(jnp.int32, (SGU_CHUNK, SGU_CHUNK), 1)
    w_mix = [jnp.where(col <= row, w_sgu_ref[g], 0.0).astype(BF16) for g in range(N_SGU_GROUPS)]

    subs = [slice(r0, r0 + sub) for r0 in range(0, tm, sub)]
    hs = [(_rms(x_ref[0, rows, :], g_pre_ref[...]) * (1.0 + scale) + shift).astype(BF16)
          for rows in subs]
    projs = [_dot(h, w_in_ref[...]) for h in hs]
    for rows, proj in zip(subs, projs):
        for p in range(aw // LANES):
            lanes = slice(p * LANES, (p + 1) * LANES)
            q_ref[0, p, rows, :] = (proj[:, 0:aw][:, lanes] * Q_SCALE).astype(BF16)
            k_ref[0, p, rows, :] = proj[:, aw:2 * aw][:, lanes].astype(BF16)
            v_ref[0, p, rows, :] = proj[:, 2 * aw:3 * aw][:, lanes].astype(BF16)

        u = jax.nn.gelu(proj[:, 3 * aw:3 * aw + N_SGU_GROUPS * gd])
        vg = jax.nn.gelu(proj[:, 3 * aw + N_SGU_GROUPS * gd:])
        for g in range(N_SGU_GROUPS):
            lo = g * gd
            vgg = vg[:, lo:lo + gd]
            d = vgg - jnp.mean(vgg, axis=-1, keepdims=True)
            yn = (d * lax.rsqrt(jnp.mean(d * d, axis=-1, keepdims=True) + EPS)
                  * g_ln_ref[:, lo:lo + gd]).astype(BF16)
            rhs = jnp.concatenate(
                [yn[n * SGU_CHUNK:(n + 1) * SGU_CHUNK, :] for n in range(n_chunks)], axis=1)
            mixed = _dot(w_mix[g], rhs) + b_sgu_ref[:, g:g + 1]
            for n in range(n_chunks):
                r0 = rows.start + n * SGU_CHUNK
                sgu_scr[r0:r0 + SGU_CHUNK, lo:lo + gd] = (
                    u[n * SGU_CHUNK:(n + 1) * SGU_CHUNK, lo:lo + gd]
                    * mixed[:, n * gd:(n + 1) * gd])
        sgu_ref[0, rows, :] = _rms(sgu_scr[rows, :], g_sgu_out_ref[...]).astype(BF16)


def _in_proj(x, mod, g_pre, w_in, g_ln, w_sgu, b_sgu_t, g_sgu_out, tm, sub):
    bsz, s, d = x.shape
    in_w = w_in.shape[1]
    aw = ATTN_WIDTH
    sw = in_w - 3 * aw
    sgu_w = sw // 2
    row_spec = lambda w: pl.BlockSpec((1, tm, w), lambda b, t: (b, t, 0))
    full = lambda shape: pl.BlockSpec(shape, lambda b, t: (0,) * len(shape),
                                      pipeline_mode=pl.Buffered(1))
    n_pairs = aw // LANES
    pair_sds = jax.ShapeDtypeStruct((bsz, n_pairs, s, LANES), BF16)
    pair_spec = pl.BlockSpec((1, n_pairs, tm, LANES), lambda b, t: (b, 0, t, 0))
    return pl.pallas_call(
        functools.partial(_in_proj_kernel, sub=sub),
        grid=(bsz, s // tm),
        in_specs=[
            row_spec(d),
            pl.BlockSpec((1, N_MOD, d), lambda b, t: (b, 0, 0)),
            full((1, d)),
            full((d, in_w)),
            full((1, sgu_w)),
            full((N_SGU_GROUPS, SGU_CHUNK, SGU_CHUNK)),
            full((SGU_CHUNK, N_SGU_GROUPS)),
            full((1, sgu_w)),
        ],
        out_specs=[pair_spec, pair_spec, pair_spec, row_spec(sgu_w)],
        out_shape=[pair_sds, pair_sds, pair_sds, jax.ShapeDtypeStruct((bsz, s, sgu_w), BF16)],
        scratch_shapes=[pltpu.VMEM((tm, sgu_w), F32)],
        compiler_params=pltpu.CompilerParams(
            dimension_semantics=("parallel", "parallel"), vmem_limit_bytes=VMEM_LIMIT_BYTES),
        name="in_proj",
    )(x, mod, g_pre, w_in, g_ln, w_sgu, b_sgu_t, g_sgu_out)


def _topk_bias_t(g, n_past):
    nrow = lax.broadcasted_iota(jnp.int32, g.shape, 0)
    beaten = jnp.zeros(g.shape, F32)
    for other in range(n_past):
        b = g[other:other + 1, :]
        beaten = beaten + jnp.where(b > g, 1.0, 0.0)
        beaten = beaten + jnp.where(jnp.logical_and(b == g, nrow > other), 1.0, 0.0)
    return jnp.where(nrow < n_past, jnp.where(beaten < MOBA_TOPK, 0.0, NEG), 0.0)


def _moba_kernel(q_ref, k_ref, v_ref, o_ref, qt_ext, k_ext, vt_ext, *, chunk):
    seq = k_ref.shape[2]
    blk = MOBA_BLOCK
    nb = seq // blk
    half = HEAD_DIM
    lane = lax.broadcasted_iota(jnp.int32, (blk, LANES), 1)
    in_head = (lane < half, lane >= half)
    bias_base = (half, 0)

    trow = lax.broadcasted_iota(jnp.int32, (LANES, blk), 0)
    in_head_t = (trow < half, trow >= half)
    v_rows = (slice(0, half + DEN_ROWS), slice(half - DEN_ROWS, LANES))
    num_rows = (slice(0, half), slice(DEN_ROWS, DEN_ROWS + half))
    den_row = (half, 0)

    for j in range(nb):
        rows = slice(j * blk, (j + 1) * blk)
        kb = k_ref[0, 0, rows, :].astype(F32)
        vb_t = v_ref[0, 0, rows, :].astype(F32).T
        for hd in range(2):
            onehot = jnp.where(lane == bias_base[hd] + j, 1.0, 0.0)
            k_ext[hd, rows, :] = jnp.where(in_head[hd], kb, onehot).astype(BF16)
            vt_ext[hd, :, rows] = jnp.where(in_head_t[hd], vb_t, 1.0).astype(BF16)

    nb_rows = -(-nb // 8) * 8
    prow = lax.broadcasted_iota(jnp.int32, (nb_rows, seq), 0)
    pcol = lax.broadcasted_iota(jnp.int32, (nb_rows, seq), 1)
    pblk = lax.shift_right_logical(pcol, blk.bit_length() - 1)
    psel = jnp.where(prow == pblk, 1.0 / blk, 0.0).astype(BF16)
    km_pair = _dot(psel, k_ref[0, 0])
    klane = lax.broadcasted_iota(jnp.int32, (nb_rows, LANES), 1)
    km = (jnp.where(klane < half, km_pair, 0.0).astype(BF16),
          jnp.where(klane >= half, km_pair, 0.0).astype(BF16))

    key_pos = lax.broadcasted_iota(jnp.int32, (blk, blk), 0)
    q_pos = lax.broadcasted_iota(jnp.int32, (blk, blk), 1)
    causal = key_pos <= q_pos

    zeros_half = jnp.zeros((half, blk), F32)
    for i in range(nb):
        cols = slice(i * blk, (i + 1) * blk)
        q_t = q_ref[0, 0, cols, :].astype(F32).T
        q_own = (q_t[:half, :], q_t[half:, :])
        for hd in range(2):
            other = zeros_half
            if i > 0:
                pair = [q_own[0], zeros_half] if hd == 0 else [zeros_half, q_own[1]]
                rows = -(-i // 8) * 8
                gate_t = _dot(km[hd][:rows, :], jnp.concatenate(pair, axis=0).astype(BF16))
                other = jnp.concatenate(
                    [_topk_bias_t(gate_t, i), jnp.zeros((half - rows, blk), F32)], axis=0)
            pair = [q_own[0], other] if hd == 0 else [other, q_own[1]]
            qt_ext[hd, :, cols] = jnp.concatenate(pair, axis=0).astype(BF16)

    items = [(i, max(c1 - chunk, 0), c1)
             for i in range(nb) for c1 in range((i + 1) * blk, 0, -chunk)]

    def scores(item):
        i, c0, c1 = item
        return [_dot(k_ext[hd, c0:c1, :], qt_ext[hd, :, i * blk:(i + 1) * blk])
                for hd in range(2)]

    s_next = scores(items[0])
    m = [None, None]
    acc = [None, None]
    for n, (i, c0, c1) in enumerate(items):
        s_cur = s_next
        if n + 1 < len(items):
            s_next = scores(items[n + 1])
        cw = c1 - c0
        for hd in range(2):
            s = s_cur[hd]
            if c1 == (i + 1) * blk:
                own = jnp.where(causal, s[cw - blk:, :], NEG)
                s = own if cw == blk else jnp.concatenate([s[:cw - blk, :], own], axis=0)
            m_chunk = jnp.max(jnp.max(s.reshape(cw // 8, 8, blk), axis=0), axis=0, keepdims=True)
            m_new = m_chunk if m[hd] is None else jnp.maximum(m[hd], m_chunk)
            pv = _dot(vt_ext[hd, v_rows[hd], c0:c1], jnp.exp2(s - m_new).astype(BF16))
            acc[hd] = pv if acc[hd] is None else acc[hd] * jnp.exp2(m[hd] - m_new) + pv
            m[hd] = m_new
        if c0 == 0:
            out_t = jnp.concatenate(
                [acc[hd][num_rows[hd], :] / acc[hd][den_row[hd]:den_row[hd] + 1, :]
                 for hd in range(2)], axis=0)
            o_ref[0, 0, i * blk:(i + 1) * blk, :] = out_t.T.astype(o_ref.dtype)
            m, acc = [None, None], [None, None]


def _moba(q, k, v, chunk):
    bsz, n_pairs, s, _ = q.shape
    spec = pl.BlockSpec((1, 1, s, LANES), lambda b, p: (b, p, 0, 0))
    return pl.pallas_call(
        functools.partial(_moba_kernel, chunk=chunk),
        grid=(bsz, n_pairs),
        in_specs=[spec, spec, spec],
        out_specs=spec,
        out_shape=jax.ShapeDtypeStruct(q.shape, BF16),
        scratch_shapes=[
            pltpu.VMEM((2, LANES, s), BF16),
            pltpu.VMEM((2, s, LANES), BF16),
            pltpu.VMEM((2, LANES, s), BF16),
        ],
        compiler_params=pltpu.CompilerParams(
            dimension_semantics=("parallel", "parallel"), vmem_limit_bytes=VMEM_LIMIT_BYTES),
        name="moba",
    )(q, k, v)


def _mix_ffn_kernel(attn_ref, sgu_ref, x_ref, mod_ref, g_attn_ref, w_out_ref, g_post_mix_ref,
                    g_pre_ref, w_gu_ref, w_down_ref, g_post_ref, o_ref, x_scr, act_scr, *,
                    sub, chunk):
    tm = x_ref.shape[1]
    aw = attn_ref.shape[1] * attn_ref.shape[3]
    d_ff = w_down_ref.shape[0]
    gate_m = mod_ref[0, 2:3, :]
    shift = mod_ref[0, 3:4, :]
    scale = mod_ref[0, 4:5, :]
    gate_f = mod_ref[0, 5:6, :]

    subs = [slice(r0, r0 + sub) for r0 in range(0, tm, sub)]
    hs = []
    for rows in subs:
        attn = jnp.concatenate(
            [attn_ref[0, p, rows, :].astype(F32) for p in range(attn_ref.shape[1])], axis=1)
        a = _rms(attn, g_attn_ref[...]).astype(BF16)
        mix = _dot(a, w_out_ref[0:aw, :]) + _dot(sgu_ref[0, rows, :], w_out_ref[aw:, :])
        x = x_ref[0, rows, :] + gate_m * _rms(mix, g_post_mix_ref[...])
        x_scr[rows, :] = x
        hs.append((_rms(x, g_pre_ref[...]) * (1.0 + scale) + shift).astype(BF16))
    for rows, h in zip(subs, hs):
        for c0 in range(0, d_ff, chunk):
            g = _dot(h, w_gu_ref[:, c0:c0 + chunk])
            up = _dot(h, w_gu_ref[:, d_ff + c0:d_ff + c0 + chunk])
            act_scr[rows, c0:c0 + chunk] = (g * jax.nn.sigmoid(g) * up).astype(BF16)
    for rows in subs:
        y = _dot(act_scr[rows, :], w_down_ref[...])
        o_ref[0, rows, :] = x_scr[rows, :] + gate_f * _rms(y, g_post_ref[...])


def _mix_ffn(attn, sgu, x, mod, g_attn, w_out, g_post_mix, g_pre, w_gu, w_down, g_post, tm, sub,
             chunk):
    bsz, s, d = x.shape
    n_pairs = attn.shape[1]
    aw = n_pairs * attn.shape[3]
    d_ff = w_down.shape[0]
    row_spec = lambda w: pl.BlockSpec((1, tm, w), lambda b, t: (b, t, 0))
    full = lambda shape: pl.BlockSpec(shape, lambda b, t: (0,) * len(shape),
                                      pipeline_mode=pl.Buffered(1))
    return pl.pallas_call(
        functools.partial(_mix_ffn_kernel, sub=sub, chunk=chunk),
        grid=(bsz, s // tm),
        in_specs=[
            pl.BlockSpec((1, n_pairs, tm, attn.shape[3]), lambda b, t: (b, 0, t, 0)),
            row_spec(sgu.shape[2]), row_spec(d),
            pl.BlockSpec((1, N_MOD, d), lambda b, t: (b, 0, 0)),
            full((1, aw)), full(w_out.shape), full((1, d)),
            full((1, d)), full(w_gu.shape), full(w_down.shape), full((1, d)),
        ],
        out_specs=row_spec(d),
        out_shape=jax.ShapeDtypeStruct((bsz, s, d), F32),
        scratch_shapes=[pltpu.VMEM((tm, d), F32), pltpu.VMEM((tm, d_ff), BF16)],
        compiler_params=pltpu.CompilerParams(
            dimension_semantics=("parallel", "parallel"), vmem_limit_bytes=VMEM_LIMIT_BYTES),
        name="mix_ffn",
    )(attn, sgu, x, mod, g_attn, w_out, g_post_mix, g_pre, w_gu, w_down, g_post)


def kernel(x, c, w_ada, b_ada, g_pre_mix, g_post_mix, w_in, g_sgu_norm, w_sgu, b_sgu,
           g_attn_out, g_sgu_out, w_out, g_pre_ffn, g_post_ffn, w_gate_up, w_down):
    bsz, s, d = x.shape
    depth = w_ada.shape[0]
    d_ff = w_down.shape[1]
    assert s % MOBA_BLOCK == 0 and s // MOBA_BLOCK <= HEAD_DIM
    assert w_in.shape[2] == 3 * ATTN_WIDTH + 2 * N_SGU_GROUPS * LANES
    tm_in, sub_in, tm_ffn, sub_ffn = 1024, 256, 1024, 256
    ffn_chunk = 256
    moba_chunk = MOBA_BLOCK
    assert d_ff % ffn_chunk == 0
    row = lambda a: a.reshape(1, -1)
    for l in range(depth):
        mod = _ada(c, w_ada[l], b_ada[l]).reshape(bsz, N_MOD, d)
        q, k, v, sgu = _in_proj(
            x, mod, row(g_pre_mix[l]), w_in[l].astype(BF16), row(g_sgu_norm[l]), w_sgu[l],
            b_sgu[l].T, row(g_sgu_out[l]), tm_in, sub_in)
        attn = _moba(q, k, v, moba_chunk)
        x = _mix_ffn(attn, sgu, x, mod, row(g_attn_out[l]), w_out[l].astype(BF16),
                     row(g_post_mix[l]), row(g_pre_ffn[l]), w_gate_up[l].astype(BF16),
                     w_down[l].astype(BF16), row(g_post_ffn[l]), tm_ffn, sub_ffn,
                     ffn_chunk)
    return x
```

```python
import functools
import math

import jax
import jax.numpy as jnp
from jax import lax
from jax.experimental import pallas as pl
from jax.experimental.pallas import tpu as pltpu

N_HEADS = 8
HEAD_DIM = 64
ATTN_WIDTH = N_HEADS * HEAD_DIM
N_SGU_GROUPS = 4
SGU_CHUNK = 128
MOBA_BLOCK = 256
MOBA_TOPK = 3
N_MOD = 6
EPS = 1e-6
NEG = -1e30

LANES = 128
DEN_ROWS = 16
VMEM_LIMIT_BYTES = 56 * 1024 * 1024
Q_SCALE = HEAD_DIM ** -0.5 * math.log2(math.e)

BF16 = jnp.bfloat16
F32 = jnp.float32


def _rms(x, g):
    return x * lax.rsqrt(jnp.mean(x * x, axis=-1, keepdims=True) + EPS) * g


def _dot(a, b):
    return jnp.dot(a, b, preferred_element_type=F32)


def _ada_kernel(c_ref, w_ref, b_ref, o_ref):
    c = c_ref[...]
    c_act = (c * jax.nn.sigmoid(c)).astype(BF16)
    o_ref[...] = _dot(c_act, w_ref[...].astype(BF16)) + b_ref[...]


def _ada(c, w_ada, b_ada):
    bsz, d = c.shape
    n = w_ada.shape[1]
    tn = d
    return pl.pallas_call(
        _ada_kernel,
        grid=(n // tn,),
        in_specs=[
            pl.BlockSpec((bsz, d), lambda j: (0, 0)),
            pl.BlockSpec((d, tn), lambda j: (0, j)),
            pl.BlockSpec((1, tn), lambda j: (0, j)),
        ],
        out_specs=pl.BlockSpec((bsz, tn), lambda j: (0, j)),
        out_shape=jax.ShapeDtypeStruct((bsz, n), F32),
        compiler_params=pltpu.CompilerParams(
            dimension_semantics=("arbitrary",), vmem_limit_bytes=VMEM_LIMIT_BYTES),
        name="ada",
    )(c, w_ada, b_ada.reshape(1, n))


def _in_proj_kernel(x_ref, mod_ref, g_pre_ref, w_in_ref, g_ln_ref, w_sgu_ref, b_sgu_ref,
                    g_sgu_out_ref, q_ref, k_ref, v_ref, sgu_ref, sgu_scr, *, sub):
    tm = x_ref.shape[1]
    aw = ATTN_WIDTH
    gd = LANES
    n_chunks = sub // SGU_CHUNK
    shift = mod_ref[0, 0:1, :]
    scale = mod_ref[0, 1:2, :]
    row = lax.broadcasted_iota(jnp.int32, (SGU_CHUNK, SGU_CHUNK), 0)
    col = lax.broadcasted_iota(jnp.int32, (SGU_CHUNK, SGU_CHUNK), 1)
    w_mix = [jnp.where(col <= row, w_sgu_ref[g], 0.0).astype(BF16) for g in range(N_SGU_GROUPS)]

    subs = [slice(r0, r0 + sub) for r0 in range(0, tm, sub)]
    hs = [(_rms(x_ref[0, rows, :], g_pre_ref[...]) * (1.0 + scale) + shift).astype(BF16)
          for rows in subs]
    projs = [_dot(h, w_in_ref[...]) for h in hs]
    for rows, proj in zip(subs, projs):
        for p in range(aw // LANES):
            lanes = slice(p * LANES, (p + 1) * LANES)
            q_ref[0, p, rows, :] = (proj[:, 0:aw][:, lanes] * Q_SCALE).astype(BF16)
            k_ref[0, p, rows, :] = proj[:, aw:2 * aw][:, lanes].astype(BF16)
            v_ref[0, p, rows, :] = proj[:, 2 * aw:3 * aw][:, lanes].astype(BF16)

        u = jax.nn.gelu(proj[:, 3 * aw:3 * aw + N_SGU_GROUPS * gd])
        vg = jax.nn.gelu(proj[:, 3 * aw + N_SGU_GROUPS * gd:])
        for g in range(N_SGU_GROUPS):
            lo = g * gd
            vgg = vg[:, lo:lo + gd]
            d = vgg - jnp.mean(vgg, axis=-1, keepdims=True)
            yn = (d * lax.rsqrt(jnp.mean(d * d, axis=-1, keepdims=True) + EPS)
                  * g_ln_ref[:, lo:lo + gd]).astype(BF16)
            rhs = jnp.concatenate(
                [yn[n * SGU_CHUNK:(n + 1) * SGU_CHUNK, :] for n in range(n_chunks)], axis=1)
            mixed = _dot(w_mix[g], rhs) + b_sgu_ref[:, g:g + 1]
            for n in range(n_chunks):
                r0 = rows.start + n * SGU_CHUNK
                sgu_scr[r0:r0 + SGU_CHUNK, lo:lo + gd] = (
                    u[n * SGU_CHUNK:(n + 1) * SGU_CHUNK, lo:lo + gd]
                    * mixed[:, n * gd:(n + 1) * gd])
        sgu_ref[0, rows, :] = _rms(sgu_scr[rows, :], g_sgu_out_ref[...]).astype(BF16)


def _in_proj(x, mod, g_pre, w_in, g_ln, w_sgu, b_sgu_t, g_sgu_out, tm, sub):
    bsz, s, d = x.shape
    in_w = w_in.shape[1]
    aw = ATTN_WIDTH
    sw = in_w - 3 * aw
    sgu_w = sw // 2
    row_spec = lambda w: pl.BlockSpec((1, tm, w), lambda b, t: (b, t, 0))
    full = lambda shape: pl.BlockSpec(shape, lambda b, t: (0,) * len(shape),
                                      pipeline_mode=pl.Buffered(1))
    n_pairs = aw // LANES
    pair_sds = jax.ShapeDtypeStruct((bsz, n_pairs, s, LANES), BF16)
    pair_spec = pl.BlockSpec((1, n_pairs, tm, LANES), lambda b, t: (b, 0, t, 0))
    return pl.pallas_call(
        functools.partial(_in_proj_kernel, sub=sub),
        grid=(bsz, s // tm),
        in_specs=[
            row_spec(d),
            pl.BlockSpec((1, N_MOD, d), lambda b, t: (b, 0, 0)),
            full((1, d)),
            full((d, in_w)),
            full((1, sgu_w)),
            full((N_SGU_GROUPS, SGU_CHUNK, SGU_CHUNK)),
            full((SGU_CHUNK, N_SGU_GROUPS)),
            full((1, sgu_w)),
        ],
        out_specs=[pair_spec, pair_spec, pair_spec, row_spec(sgu_w)],
        out_shape=[pair_sds, pair_sds, pair_sds, jax.ShapeDtypeStruct((bsz, s, sgu_w), BF16)],
        scratch_shapes=[pltpu.VMEM((tm, sgu_w), F32)],
        compiler_params=pltpu.CompilerParams(
            dimension_semantics=("parallel", "parallel"), vmem_limit_bytes=VMEM_LIMIT_BYTES),
        name="in_proj",
    )(x, mod, g_pre, w_in, g_ln, w_sgu, b_sgu_t, g_sgu_out)


def _topk_bias_t(g, n_past):
    nrow = lax.broadcasted_iota(jnp.int32, g.shape, 0)
    beaten = jnp.zeros(g.shape, F32)
    for other in range(n_past):
        b = g[other:other + 1, :]
        beaten = beaten + jnp.where(b > g, 1.0, 0.0)
        beaten = beaten + jnp.where(jnp.logical_and(b == g, nrow > other), 1.0, 0.0)
    return jnp.where(nrow < n_past, jnp.where(beaten < MOBA_TOPK, 0.0, NEG), 0.0)


def _moba_kernel(q_ref, k_ref, v_ref, o_ref, qt_ext, k_ext, vt_ext, *, chunk):
    seq = k_ref.shape[2]
    blk = MOBA_BLOCK
    nb = seq // blk
    half = HEAD_DIM
    lane = lax.broadcasted_iota(jnp.int32, (blk, LANES), 1)
    in_head = (lane < half, lane >= half)
    bias_base = (half, 0)

    trow = lax.broadcasted_iota(jnp.int32, (LANES, blk), 0)
    in_head_t = (trow < half, trow >= half)
    v_rows = (slice(0, half + DEN_ROWS), slice(half - DEN_ROWS, LANES))
    num_rows = (slice(0, half), slice(DEN_ROWS, DEN_ROWS + half))
    den_row = (half, 0)

    for j in range(nb):
        rows = slice(j * blk, (j + 1) * blk)
        kb = k_ref[0, 0, rows, :].astype(F32)
        vb_t = v_ref[0, 0, rows, :].astype(F32).T
        for hd in range(2):
            onehot = jnp.where(lane == bias_base[hd] + j, 1.0, 0.0)
            k_ext[hd, rows, :] = jnp.where(in_head[hd], kb, onehot).astype(BF16)
            vt_ext[hd, :, rows] = jnp.where(in_head_t[hd], vb_t, 1.0).astype(BF16)

    nb_rows = -(-nb // 8) * 8
    prow = lax.broadcasted_iota(jnp.int32, (nb_rows, seq), 0)
    pcol = lax.broadcasted_iota(jnp.int32, (nb_rows, seq), 1)
    pblk = lax.shift_right_logical(pcol, blk.bit_length() - 1)
    psel = jnp.where(prow == pblk, 1.0 / blk, 0.0).astype(BF16)
    km_pair = _dot(psel, k_ref[0, 0])
    klane = lax.broadcasted_iota(jnp.int32, (nb_rows, LANES), 1)
    km = (jnp.where(klane < half, km_pair, 0.0).astype(BF16),
          jnp.where(klane >= half, km_pair, 0.0).astype(BF16))

    key_pos = lax.broadcasted_iota(jnp.int32, (blk, blk), 0)
    q_pos = lax.broadcasted_iota(jnp.int32, (blk, blk), 1)
    causal = key_pos <= q_pos

    zeros_half = jnp.zeros((half, blk), F32)
    for i in range(nb):
        cols = slice(i * blk, (i + 1) * blk)
        q_t = q_ref[0, 0, cols, :].astype(F32).T
        q_own = (q_t[:half, :], q_t[half:, :])
        for hd in range(2):
            other = zeros_half
            if i > 0:
                pair = [q_own[0], zeros_half] if hd == 0 else [zeros_half, q_own[1]]
                rows = -(-i // 8) * 8
                gate_t = _dot(km[hd][:rows, :], jnp.concatenate(pair, axis=0).astype(BF16))
                other = jnp.concatenate(
                    [_topk_bias_t(gate_t, i), jnp.zeros((half - rows, blk), F32)], axis=0)
            pair = [q_own[0], other] if hd == 0 else [other, q_own[1]]
            qt_ext[hd, :, cols] = jnp.concatenate(pair, axis=0).astype(BF16)

    items = [(i, max(c1 - chunk, 0), c1)
             for i in range(nb) for c1 in range((i + 1) * blk, 0, -chunk)]

    def scores(item):
        i, c0, c1 = item
        return [_dot(k_ext[hd, c0:c1, :], qt_ext[hd, :, i * blk:(i + 1) * blk])
                for hd in range(2)]

    s_next = scores(items[0])
    m = [None, None]
    acc = [None, None]
    for n, (i, c0, c1) in enumerate(items):
        s_cur = s_next
        if n + 1 < len(items):
            s_next = scores(items[n + 1])
        cw = c1 - c0
        for hd in range(2):
            s = s_cur[hd]
            if c1 == (i + 1) * blk:
                own = jnp.where(causal, s[cw - blk:, :], NEG)
                s = own if cw == blk else jnp.concatenate([s[:cw - blk, :], own], axis=0)
            m_chunk = jnp.max(jnp.max(s.reshape(cw // 8, 8, blk), axis=0), axis=0, keepdims=True)
            m_new = m_chunk if m[hd] is None else jnp.maximum(m[hd], m_chunk)
            pv = _dot(vt_ext[hd, v_rows[hd], c0:c1], jnp.exp2(s - m_new).astype(BF16))
            acc[hd] = pv if acc[hd] is None else acc[hd] * jnp.exp2(m[hd] - m_new) + pv
            m[hd] = m_new
        if c0 == 0:
            out_t = jnp.concatenate(
                [acc[hd][num_rows[hd], :] / acc[hd][den_row[hd]:den_row[hd] + 1, :]
                 for hd in range(2)], axis=0)
            o_ref[0, 0, i * blk:(i + 1) * blk, :] = out_t.T.astype(o_ref.dtype)
            m, acc = [None, None], [None, None]


def _moba(q, k, v, chunk):
    bsz, n_pairs, s, _ = q.shape
    spec = pl.BlockSpec((1, 1, s, LANES), lambda b, p: (b, p, 0, 0))
    return pl.pallas_call(
        functools.partial(_moba_kernel, chunk=chunk),
        grid=(bsz, n_pairs),
        in_specs=[spec, spec, spec],
        out_specs=spec,
        out_shape=jax.ShapeDtypeStruct(q.shape, BF16),
        scratch_shapes=[
            pltpu.VMEM((2, LANES, s), BF16),
            pltpu.VMEM((2, s, LANES), BF16),
            pltpu.VMEM((2, LANES, s), BF16),
        ],
        compiler_params=pltpu.CompilerParams(
            dimension_semantics=("parallel", "parallel"), vmem_limit_bytes=VMEM_LIMIT_BYTES),
        name="moba",
    )(q, k, v)


def _mix_ffn_kernel(attn_ref, sgu_ref, x_ref, mod_ref, g_attn_ref, w_out_ref, g_post_mix_ref,
                    g_pre_ref, w_gu_ref, w_down_ref, g_post_ref, o_ref, x_scr, act_scr, *,
                    sub, chunk):
    tm = x_ref.shape[1]
    aw = attn_ref.shape[1] * attn_ref.shape[3]
    d_ff = w_down_ref.shape[0]
    gate_m = mod_ref[0, 2:3, :]
    shift = mod_ref[0, 3:4, :]
    scale = mod_ref[0, 4:5, :]
    gate_f = mod_ref[0, 5:6, :]

    subs = [slice(r0, r0 + sub) for r0 in range(0, tm, sub)]
    hs = []
    for rows in subs:
        attn = jnp.concatenate(
            [attn_ref[0, p, rows, :].astype(F32) for p in range(attn_ref.shape[1])], axis=1)
        a = _rms(attn, g_attn_ref[...]).astype(BF16)
        mix = _dot(a, w_out_ref[0:aw, :]) + _dot(sgu_ref[0, rows, :], w_out_ref[aw:, :])
        x = x_ref[0, rows, :] + gate_m * _rms(mix, g_post_mix_ref[...])
        x_scr[rows, :] = x
        hs.append((_rms(x, g_pre_ref[...]) * (1.0 + scale) + shift).astype(BF16))
    for rows, h in zip(subs, hs):
        for c0 in range(0, d_ff, chunk):
            g = _dot(h, w_gu_ref[:, c0:c0 + chunk])
            up = _dot(h, w_gu_ref[:, d_ff + c0:d_ff + c0 + chunk])
            act_scr[rows, c0:c0 + chunk] = (g * jax.nn.sigmoid(g) * up).astype(BF16)
    for rows in subs:
        y = _dot(act_scr[rows, :], w_down_ref[...])
        o_ref[0, rows, :] = x_scr[rows, :] + gate_f * _rms(y, g_post_ref[...])


def _mix_ffn(attn, sgu, x, mod, g_attn, w_out, g_post_mix, g_pre, w_gu, w_down, g_post, tm, sub,
             chunk):
    bsz, s, d = x.shape
    n_pairs = attn.shape[1]
    aw = n_pairs * attn.shape[3]
    d_ff = w_down.shape[0]
    row_spec = lambda w: pl.BlockSpec((1, tm, w), lambda b, t: (b, t, 0))
    full = lambda shape: pl.BlockSpec(shape, lambda b, t: (0,) * len(shape),
                                      pipeline_mode=pl.Buffered(1))
    return pl.pallas_call(
        functools.partial(_mix_ffn_kernel, sub=sub, chunk=chunk),
        grid=(bsz, s // tm),
        in_specs=[
            pl.BlockSpec((1, n_pairs, tm, attn.shape[3]), lambda b, t: (b, 0, t, 0)),
            row_spec(sgu.shape[2]), row_spec(d),
            pl.BlockSpec((1, N_MOD, d), lambda b, t: (b, 0, 0)),
            full((1, aw)), full(w_out.shape), full((1, d)),
            full((1, d)), full(w_gu.shape), full(w_down.shape), full((1, d)),
        ],
        out_specs=row_spec(d),
        out_shape=jax.ShapeDtypeStruct((bsz, s, d), F32),
        scratch_shapes=[pltpu.VMEM((tm, d), F32), pltpu.VMEM((tm, d_ff), BF16)],
        compiler_params=pltpu.CompilerParams(
            dimension_semantics=("parallel", "parallel"), vmem_limit_bytes=VMEM_LIMIT_BYTES),
        name="mix_ffn",
    )(attn, sgu, x, mod, g_attn, w_out, g_post_mix, g_pre, w_gu, w_down, g_post)


def kernel(x, c, w_ada, b_ada, g_pre_mix, g_post_mix, w_in, g_sgu_norm, w_sgu, b_sgu,
           g_attn_out, g_sgu_out, w_out, g_pre_ffn, g_post_ffn, w_gate_up, w_down):
    bsz, s, d = x.shape
    depth = w_ada.shape[0]
    d_ff = w_down.shape[1]
    assert s % MOBA_BLOCK == 0 and s // MOBA_BLOCK <= HEAD_DIM
    assert w_in.shape[2] == 3 * ATTN_WIDTH + 2 * N_SGU_GROUPS * LANES
    tm_in, sub_in, tm_ffn, sub_ffn = 2048, 256, 1024, 256
    ffn_chunk = 256
    moba_chunk = 2 * MOBA_BLOCK
    assert d_ff % ffn_chunk == 0
    row = lambda a: a.reshape(1, -1)
    for l in range(depth):
        mod = _ada(c, w_ada[l], b_ada[l]).reshape(bsz, N_MOD, d)
        q, k, v, sgu = _in_proj(
            x, mod, row(g_pre_mix[l]), w_in[l].astype(BF16), row(g_sgu_norm[l]), w_sgu[l],
            b_sgu[l].T, row(g_sgu_out[l]), tm_in, sub_in)
        attn = _moba(q, k, v, moba_chunk)
        x = _mix_ffn(attn, sgu, x, mod, row(g_attn_out[l]), w_out[l].astype(BF16),
                     row(g_post_mix[l]), row(g_pre_ffn[l]), w_gate_up[l].astype(BF16),
                     w_down[l].astype(BF16), row(g_post_ffn[l]), tm_ffn, sub_ffn,
                     ffn_chunk)
    return x
```

```python
import functools
import math

import jax
import jax.numpy as jnp
from jax import lax
from jax.experimental import pallas as pl
from jax.experimental.pallas import tpu as pltpu

N_HEADS = 8
HEAD_DIM = 64
ATTN_WIDTH = N_HEADS * HEAD_DIM
N_SGU_GROUPS = 4
SGU_CHUNK = 128
MOBA_BLOCK = 256
MOBA_TOPK = 3
N_MOD = 6
EPS = 1e-6
NEG = -1e30

LANES = 128
DEN_ROWS = 16
VMEM_LIMIT_BYTES = 56 * 1024 * 1024
Q_SCALE = HEAD_DIM ** -0.5 * math.log2(math.e)

BF16 = jnp.bfloat16
F32 = jnp.float32


def _rms(x, g):
    return x * lax.rsqrt(jnp.mean(x * x, axis=-1, keepdims=True) + EPS) * g


def _dot(a, b):
    return jnp.dot(a, b, preferred_element_type=F32)


def _ada_kernel(c_ref, w_ref, b_ref, o_ref):
    c = c_ref[...]
    c_act = (c * jax.nn.sigmoid(c)).astype(BF16)
    o_ref[...] = _dot(c_act, w_ref[...].astype(BF16)) + b_ref[...]


def _ada(c, w_ada, b_ada):
    bsz, d = c.shape
    n = w_ada.shape[1]
    tn = d
    return pl.pallas_call(
        _ada_kernel,
        grid=(n // tn,),
        in_specs=[
            pl.BlockSpec((bsz, d), lambda j: (0, 0)),
            pl.BlockSpec((d, tn), lambda j: (0, j)),
            pl.BlockSpec((1, tn), lambda j: (0, j)),
        ],
        out_specs=pl.BlockSpec((bsz, tn), lambda j: (0, j)),
        out_shape=jax.ShapeDtypeStruct((bsz, n), F32),
        compiler_params=pltpu.CompilerParams(
            dimension_semantics=("arbitrary",), vmem_limit_bytes=VMEM_LIMIT_BYTES),
        name="ada",
    )(c, w_ada, b_ada.reshape(1, n))


def _in_proj_kernel(*refs, sub, n_cast):
    (x_ref, mod_ref, g_pre_ref, w_in_ref, g_ln_ref, w_sgu_ref, b_sgu_ref,
     g_sgu_out_ref) = refs[:8]
    cast_in = refs[8:8 + n_cast]
    q_ref, k_ref, v_ref, sgu_ref = refs[8 + n_cast:12 + n_cast]
    cast_out = refs[12 + n_cast:12 + 2 * n_cast]
    sgu_scr = refs[12 + 2 * n_cast]
    for w_ref, o_ref in zip(cast_in, cast_out):
        o_ref[...] = w_ref[...].astype(BF16)

    tm = x_ref.shape[1]
    aw = ATTN_WIDTH
    gd = LANES
    n_chunks = sub // SGU_CHUNK
    shift = mod_ref[0, 0:1, :]
    scale = mod_ref[0, 1:2, :]
    row = lax.broadcasted_iota(jnp.int32, (SGU_CHUNK, SGU_CHUNK), 0)
    col = lax.broadcasted_iota(jnp.int32, (SGU_CHUNK, SGU_CHUNK), 1)
    w_mix = [jnp.where(col <= row, w_sgu_ref[g], 0.0).astype(BF16) for g in range(N_SGU_GROUPS)]

    subs = [slice(r0, r0 + sub) for r0 in range(0, tm, sub)]
    hs = [(_rms(x_ref[0, rows, :], g_pre_ref[...]) * (1.0 + scale) + shift).astype(BF16)
          for rows in subs]
    projs = [_dot(h, w_in_ref[...]) for h in hs]
    for rows, proj in zip(subs, projs):
        for p in range(aw // LANES):
            lanes = slice(p * LANES, (p + 1) * LANES)
            q_ref[0, p, rows, :] = (proj[:, 0:aw][:, lanes] * Q_SCALE).astype(BF16)
            k_ref[0, p, rows, :] = proj[:, aw:2 * aw][:, lanes].astype(BF16)
            v_ref[0, p, rows, :] = proj[:, 2 * aw:3 * aw][:, lanes].astype(BF16)

        u = jax.nn.gelu(proj[:, 3 * aw:3 * aw + N_SGU_GROUPS * gd])
        vg = jax.nn.gelu(proj[:, 3 * aw + N_SGU_GROUPS * gd:])
        for g in range(N_SGU_GROUPS):
            lo = g * gd
            vgg = vg[:, lo:lo + gd]
            d = vgg - jnp.mean(vgg, axis=-1, keepdims=True)
            yn = (d * lax.rsqrt(jnp.mean(d * d, axis=-1, keepdims=True) + EPS)
                  * g_ln_ref[:, lo:lo + gd]).astype(BF16)
            rhs = jnp.concatenate(
                [yn[n * SGU_CHUNK:(n + 1) * SGU_CHUNK, :] for n in range(n_chunks)], axis=1)
            mixed = _dot(w_mix[g], rhs) + b_sgu_ref[:, g:g + 1]
            for n in range(n_chunks):
                r0 = rows.start + n * SGU_CHUNK
                sgu_scr[r0:r0 + SGU_CHUNK, lo:lo + gd] = (
                    u[n * SGU_CHUNK:(n + 1) * SGU_CHUNK, lo:lo + gd]
                    * mixed[:, n * gd:(n + 1) * gd])
        sgu_ref[0, rows, :] = _rms(sgu_scr[rows, :], g_sgu_out_ref[...]).astype(BF16)


def _in_proj(x, mod, g_pre, w_in, g_ln, w_sgu, b_sgu_t, g_sgu_out, cast_ws, tm, sub):
    bsz, s, d = x.shape
    n_t = s // tm
    n_steps = bsz * n_t
    cast_views = [w.reshape(n_steps, w.shape[0] // n_steps, w.shape[1]) for w in cast_ws]
    cast_specs = [pl.BlockSpec((1,) + w.shape[1:], lambda b, t: (b * n_t + t, 0, 0))
                  for w in cast_views]
    in_w = w_in.shape[1]
    aw = ATTN_WIDTH
    sw = in_w - 3 * aw
    sgu_w = sw // 2
    row_spec = lambda w: pl.BlockSpec((1, tm, w), lambda b, t: (b, t, 0))
    full = lambda shape: pl.BlockSpec(shape, lambda b, t: (0,) * len(shape),
                                      pipeline_mode=pl.Buffered(1))
    n_pairs = aw // LANES
    pair_sds = jax.ShapeDtypeStruct((bsz, n_pairs, s, LANES), BF16)
    pair_spec = pl.BlockSpec((1, n_pairs, tm, LANES), lambda b, t: (b, 0, t, 0))
    outs = pl.pallas_call(
        functools.partial(_in_proj_kernel, sub=sub, n_cast=len(cast_ws)),
        grid=(bsz, n_t),
        in_specs=[
            row_spec(d),
            pl.BlockSpec((1, N_MOD, d), lambda b, t: (b, 0, 0)),
            full((1, d)),
            full((d, in_w)),
            full((1, sgu_w)),
            full((N_SGU_GROUPS, SGU_CHUNK, SGU_CHUNK)),
            full((SGU_CHUNK, N_SGU_GROUPS)),
            full((1, sgu_w)),
        ] + cast_specs,
        out_specs=[pair_spec, pair_spec, pair_spec, row_spec(sgu_w)] + cast_specs,
        out_shape=[pair_sds, pair_sds, pair_sds, jax.ShapeDtypeStruct((bsz, s, sgu_w), BF16)]
        + [jax.ShapeDtypeStruct(w.shape, BF16) for w in cast_views],
        scratch_shapes=[pltpu.VMEM((tm, sgu_w), F32)],
        compiler_params=pltpu.CompilerParams(
            dimension_semantics=("parallel", "parallel"), vmem_limit_bytes=VMEM_LIMIT_BYTES),
        name="in_proj",
    )(x, mod, g_pre, w_in, g_ln, w_sgu, b_sgu_t, g_sgu_out, *cast_views)
    return outs[:4], [o.reshape(w.shape) for o, w in zip(outs[4:], cast_ws)]


def _topk_bias_t(g, n_past):
    nrow = lax.broadcasted_iota(jnp.int32, g.shape, 0)
    beaten = jnp.zeros(g.shape, F32)
    for other in range(n_past):
        b = g[other:other + 1, :]
        beaten = beaten + jnp.where(b > g, 1.0, 0.0)
        beaten = beaten + jnp.where(jnp.logical_and(b == g, nrow > other), 1.0, 0.0)
    return jnp.where(nrow < n_past, jnp.where(beaten < MOBA_TOPK, 0.0, NEG), 0.0)


def _moba_kernel(q_ref, k_ref, v_ref, o_ref, qt_ext, k_ext, vt_ext, *, chunk):
    seq = k_ref.shape[2]
    blk = MOBA_BLOCK
    nb = seq // blk
    half = HEAD_DIM
    lane = lax.broadcasted_iota(jnp.int32, (blk, LANES), 1)
    in_head = (lane < half, lane >= half)
    bias_base = (half, 0)

    trow = lax.broadcasted_iota(jnp.int32, (LANES, blk), 0)
    in_head_t = (trow < half, trow >= half)
    v_rows = (slice(0, half + DEN_ROWS), slice(half - DEN_ROWS, LANES))
    num_rows = (slice(0, half), slice(DEN_ROWS, DEN_ROWS + half))
    den_row = (half, 0)

    for j in range(nb):
        rows = slice(j * blk, (j + 1) * blk)
        kb = k_ref[0, 0, rows, :].astype(F32)
        vb_t = v_ref[0, 0, rows, :].astype(F32).T
        for hd in range(2):
            onehot = jnp.where(lane == bias_base[hd] + j, 1.0, 0.0)
            k_ext[hd, rows, :] = jnp.where(in_head[hd], kb, onehot).astype(BF16)
            vt_ext[hd, :, rows] = jnp.where(in_head_t[hd], vb_t, 1.0).astype(BF16)

    nb_rows = -(-nb // 8) * 8
    prow = lax.broadcasted_iota(jnp.int32, (nb_rows, seq), 0)
    pcol = lax.broadcasted_iota(jnp.int32, (nb_rows, seq), 1)
    pblk = lax.shift_right_logical(pcol, blk.bit_length() - 1)
    psel = jnp.where(prow == pblk, 1.0 / blk, 0.0).astype(BF16)
    km_pair = _dot(psel, k_ref[0, 0])
    klane = lax.broadcasted_iota(jnp.int32, (nb_rows, LANES), 1)
    km = (jnp.where(klane < half, km_pair, 0.0).astype(BF16),
          jnp.where(klane >= half, km_pair, 0.0).astype(BF16))

    key_pos = lax.broadcasted_iota(jnp.int32, (blk, blk), 0)
    q_pos = lax.broadcasted_iota(jnp.int32, (blk, blk), 1)
    causal = key_pos <= q_pos

    zeros_half = jnp.zeros((half, blk), F32)
    for i in range(nb):
        cols = slice(i * blk, (i + 1) * blk)
        q_t = q_ref[0, 0, cols, :].astype(F32).T
        q_own = (q_t[:half, :], q_t[half:, :])
        for hd in range(2):
            other = zeros_half
            if i > 0:
                pair = [q_own[0], zeros_half] if hd == 0 else [zeros_half, q_own[1]]
                rows = -(-i // 8) * 8
                gate_t = _dot(km[hd][:rows, :], jnp.concatenate(pair, axis=0).astype(BF16))
                other = jnp.concatenate(
                    [_topk_bias_t(gate_t, i), jnp.zeros((half - rows, blk), F32)], axis=0)
            pair = [q_own[0], other] if hd == 0 else [other, q_own[1]]
            qt_ext[hd, :, cols] = jnp.concatenate(pair, axis=0).astype(BF16)

    items = [(i, max(c1 - chunk, 0), c1)
             for i in range(nb) for c1 in range((i + 1) * blk, 0, -chunk)]

    def scores(item):
        i, c0, c1 = item
        return [_dot(k_ext[hd, c0:c1, :], qt_ext[hd, :, i * blk:(i + 1) * blk])
                for hd in range(2)]

    s_next = scores(items[0])
    m = [None, None]
    acc = [None, None]
    for n, (i, c0, c1) in enumerate(items):
        s_cur = s_next
        if n + 1 < len(items):
            s_next = scores(items[n + 1])
        cw = c1 - c0
        for hd in range(2):
            s = s_cur[hd]
            if c1 == (i + 1) * blk:
                own = jnp.where(causal, s[cw - blk:, :], NEG)
                s = own if cw == blk else jnp.concatenate([s[:cw - blk, :], own], axis=0)
            m_chunk = jnp.max(jnp.max(s.reshape(cw // 8, 8, blk), axis=0), axis=0, keepdims=True)
            m_new = m_chunk if m[hd] is None else jnp.maximum(m[hd], m_chunk)
            pv = _dot(vt_ext[hd, v_rows[hd], c0:c1], jnp.exp2(s - m_new).astype(BF16))
            acc[hd] = pv if acc[hd] is None else acc[hd] * jnp.exp2(m[hd] - m_new) + pv
            m[hd] = m_new
        if c0 == 0:
            out_t = jnp.concatenate(
                [acc[hd][num_rows[hd], :] / acc[hd][den_row[hd]:den_row[hd] + 1, :]
                 for hd in range(2)], axis=0)
            o_ref[0, 0, i * blk:(i + 1) * blk, :] = out_t.T.astype(o_ref.dtype)
            m, acc = [None, None], [None, None]


def _moba(q, k, v, chunk):
    bsz, n_pairs, s, _ = q.shape
    spec = pl.BlockSpec((1, 1, s, LANES), lambda b, p: (b, p, 0, 0))
    return pl.pallas_call(
        functools.partial(_moba_kernel, chunk=chunk),
        grid=(bsz, n_pairs),
        in_specs=[spec, spec, spec],
        out_specs=spec,
        out_shape=jax.ShapeDtypeStruct(q.shape, BF16),
        scratch_shapes=[
            pltpu.VMEM((2, LANES, s), BF16),
            pltpu.VMEM((2, s, LANES), BF16),
            pltpu.VMEM((2, LANES, s), BF16),
        ],
        compiler_params=pltpu.CompilerParams(
            dimension_semantics=("parallel", "parallel"), vmem_limit_bytes=VMEM_LIMIT_BYTES),
        name="moba",
    )(q, k, v)


def _mix_ffn_kernel(attn_ref, sgu_ref, x_ref, mod_ref, g_attn_ref, w_out_ref, g_post_mix_ref,
                    g_pre_ref, w_gu_ref, w_down_ref, g_post_ref, o_ref, x_scr, act_scr, *,
                    sub, chunk):
    tm = x_ref.shape[1]
    aw = attn_ref.shape[1] * attn_ref.shape[3]
    d_ff = w_down_ref.shape[0]
    gate_m = mod_ref[0, 2:3, :]
    shift = mod_ref[0, 3:4, :]
    scale = mod_ref[0, 4:5, :]
    gate_f = mod_ref[0, 5:6, :]

    subs = [slice(r0, r0 + sub) for r0 in range(0, tm, sub)]
    hs = []
    for rows in subs:
        attn = jnp.concatenate(
            [attn_ref[0, p, rows, :].astype(F32) for p in range(attn_ref.shape[1])], axis=1)
        a = _rms(attn, g_attn_ref[...]).astype(BF16)
        mix = _dot(a, w_out_ref[0:aw, :]) + _dot(sgu_ref[0, rows, :], w_out_ref[aw:, :])
        x = x_ref[0, rows, :] + gate_m * _rms(mix, g_post_mix_ref[...])
        x_scr[rows, :] = x
        hs.append((_rms(x, g_pre_ref[...]) * (1.0 + scale) + shift).astype(BF16))
    for rows, h in zip(subs, hs):
        for c0 in range(0, d_ff, chunk):
            g = _dot(h, w_gu_ref[:, c0:c0 + chunk])
            up = _dot(h, w_gu_ref[:, d_ff + c0:d_ff + c0 + chunk])
            act_scr[rows, c0:c0 + chunk] = (g * jax.nn.sigmoid(g) * up).astype(BF16)
    for rows in subs:
        y = _dot(act_scr[rows, :], w_down_ref[...])
        o_ref[0, rows, :] = x_scr[rows, :] + gate_f * _rms(y, g_post_ref[...])


def _mix_ffn(attn, sgu, x, mod, g_attn, w_out, g_post_mix, g_pre, w_gu, w_down, g_post, tm, sub,
             chunk):
    bsz, s, d = x.shape
    n_pairs = attn.shape[1]
    aw = n_pairs * attn.shape[3]
    d_ff = w_down.shape[0]
    row_spec = lambda w: pl.BlockSpec((1, tm, w), lambda b, t: (b, t, 0))
    full = lambda shape: pl.BlockSpec(shape, lambda b, t: (0,) * len(shape),
                                      pipeline_mode=pl.Buffered(1))
    return pl.pallas_call(
        functools.partial(_mix_ffn_kernel, sub=sub, chunk=chunk),
        grid=(bsz, s // tm),
        in_specs=[
            pl.BlockSpec((1, n_pairs, tm, attn.shape[3]), lambda b, t: (b, 0, t, 0)),
            row_spec(sgu.shape[2]), row_spec(d),
            pl.BlockSpec((1, N_MOD, d), lambda b, t: (b, 0, 0)),
            full((1, aw)), full(w_out.shape), full((1, d)),
            full((1, d)), full(w_gu.shape), full(w_down.shape), full((1, d)),
        ],
        out_specs=row_spec(d),
        out_shape=jax.ShapeDtypeStruct((bsz, s, d), F32),
        scratch_shapes=[pltpu.VMEM((tm, d), F32), pltpu.VMEM((tm, d_ff), BF16)],
        compiler_params=pltpu.CompilerParams(
            dimension_semantics=("parallel", "parallel"), vmem_limit_bytes=VMEM_LIMIT_BYTES),
        name="mix_ffn",
    )(attn, sgu, x, mod, g_attn, w_out, g_post_mix, g_pre, w_gu, w_down, g_post)


def kernel(x, c, w_ada, b_ada, g_pre_mix, g_post_mix, w_in, g_sgu_norm, w_sgu, b_sgu,
           g_attn_out, g_sgu_out, w_out, g_pre_ffn, g_post_ffn, w_gate_up, w_down):
    bsz, s, d = x.shape
    depth = w_ada.shape[0]
    d_ff = w_down.shape[1]
    assert s % MOBA_BLOCK == 0 and s // MOBA_BLOCK <= HEAD_DIM
    assert w_in.shape[2] == 3 * ATTN_WIDTH + 2 * N_SGU_GROUPS * LANES
    tm_in, sub_in, tm_ffn, sub_ffn = 2048, 256, 1024, 256
    ffn_chunk = 256
    moba_chunk = 2 * MOBA_BLOCK
    assert d_ff % ffn_chunk == 0 and s % tm_in == 0 and s % tm_ffn == 0
    n_in_steps = bsz * (s // tm_in)
    assert d % n_in_steps == 0 and d_ff % n_in_steps == 0
    row = lambda a: a.reshape(1, -1)
    for l in range(depth):
        mod = _ada(c, w_ada[l], b_ada[l]).reshape(bsz, N_MOD, d)
        (q, k, v, sgu), (w_out_b, w_gu_b, w_down_b) = _in_proj(
            x, mod, row(g_pre_mix[l]), w_in[l].astype(BF16), row(g_sgu_norm[l]), w_sgu[l],
            b_sgu[l].T, row(g_sgu_out[l]), [w_out[l], w_gate_up[l], w_down[l]], tm_in, sub_in)
        attn = _moba(q, k, v, moba_chunk)
        x = _mix_ffn(attn, sgu, x, mod, row(g_attn_out[l]), w_out_b, row(g_post_mix[l]),
                     row(g_pre_ffn[l]), w_gu_b, w_down_b, row(g_post_ffn[l]), tm_ffn, sub_ffn,
                     ffn_chunk)
    return x
```

```python
import functools
import math

import jax
import jax.numpy as jnp
from jax import lax
from jax.experimental import pallas as pl
from jax.experimental.pallas import tpu as pltpu

N_HEADS = 8
HEAD_DIM = 64
ATTN_WIDTH = N_HEADS * HEAD_DIM
N_SGU_GROUPS = 4
SGU_CHUNK = 128
MOBA_BLOCK = 256
MOBA_TOPK = 3
N_MOD = 6
EPS = 1e-6
NEG = -1e30

LANES = 128
DEN_ROWS = 16
BOUND_HEADROOM = 60.0
BOUND_GAP = 160.0
VMEM_LIMIT_BYTES = 56 * 1024 * 1024
Q_SCALE = HEAD_DIM ** -0.5 * math.log2(math.e)

BF16 = jnp.bfloat16
F32 = jnp.float32


def _rms(x, g):
    return x * lax.rsqrt(jnp.mean(x * x, axis=-1, keepdims=True) + EPS) * g


def _dot(a, b):
    return jnp.dot(a, b, preferred_element_type=F32)


def _ada_kernel(c_ref, w_ref, b_ref, o_ref):
    c = c_ref[...]
    c_act = (c * jax.nn.sigmoid(c)).astype(BF16)
    o_ref[...] = _dot(c_act, w_ref[...].astype(BF16)) + b_ref[...]


def _ada(c, w_ada, b_ada):
    bsz, d = c.shape
    n = w_ada.shape[1]
    tn = d
    return pl.pallas_call(
        _ada_kernel,
        grid=(n // tn,),
        in_specs=[
            pl.BlockSpec((bsz, d), lambda j: (0, 0)),
            pl.BlockSpec((d, tn), lambda j: (0, j)),
            pl.BlockSpec((1, tn), lambda j: (0, j)),
        ],
        out_specs=pl.BlockSpec((bsz, tn), lambda j: (0, j)),
        out_shape=jax.ShapeDtypeStruct((bsz, n), F32),
        compiler_params=pltpu.CompilerParams(
            dimension_semantics=("arbitrary",), vmem_limit_bytes=VMEM_LIMIT_BYTES),
        name="ada",
    )(c, w_ada, b_ada.reshape(1, n))


def _in_proj_kernel(*refs, sub, n_cast):
    (x_ref, mod_ref, g_pre_ref, w_in_ref, g_ln_ref, w_sgu_ref, b_sgu_ref,
     g_sgu_out_ref) = refs[:8]
    cast_in = refs[8:8 + n_cast]
    q_ref, k_ref, v_ref, sgu_ref = refs[8 + n_cast:12 + n_cast]
    cast_out = refs[12 + n_cast:12 + 2 * n_cast]
    sgu_scr = refs[12 + 2 * n_cast]
    for w_ref, o_ref in zip(cast_in, cast_out):
        o_ref[...] = w_ref[...].astype(BF16)

    tm = x_ref.shape[1]
    aw = ATTN_WIDTH
    gd = LANES
    n_chunks = sub // SGU_CHUNK
    shift = mod_ref[0, 0:1, :]
    scale = mod_ref[0, 1:2, :]
    row = lax.broadcasted_iota(jnp.int32, (SGU_CHUNK, SGU_CHUNK), 0)
    col = lax.broadcasted_iota(jnp.int32, (SGU_CHUNK, SGU_CHUNK), 1)
    w_mix = [jnp.where(col <= row, w_sgu_ref[g], 0.0).astype(BF16) for g in range(N_SGU_GROUPS)]

    subs = [slice(r0, r0 + sub) for r0 in range(0, tm, sub)]
    hs = [(_rms(x_ref[0, rows, :], g_pre_ref[...]) * (1.0 + scale) + shift).astype(BF16)
          for rows in subs]
    projs = [_dot(h, w_in_ref[...]) for h in hs]
    for rows, proj in zip(subs, projs):
        for p in range(aw // LANES):
            lanes = slice(p * LANES, (p + 1) * LANES)
            q_ref[0, p, rows, :] = (proj[:, 0:aw][:, lanes] * Q_SCALE).astype(BF16)
            k_ref[0, p, rows, :] = proj[:, aw:2 * aw][:, lanes].astype(BF16)
            v_ref[0, p, rows, :] = proj[:, 2 * aw:3 * aw][:, lanes].astype(BF16)

        u = jax.nn.gelu(proj[:, 3 * aw:3 * aw + N_SGU_GROUPS * gd])
        vg = jax.nn.gelu(proj[:, 3 * aw + N_SGU_GROUPS * gd:])
        for g in range(N_SGU_GROUPS):
            lo = g * gd
            vgg = vg[:, lo:lo + gd]
            d = vgg - jnp.mean(vgg, axis=-1, keepdims=True)
            yn = (d * lax.rsqrt(jnp.mean(d * d, axis=-1, keepdims=True) + EPS)
                  * g_ln_ref[:, lo:lo + gd]).astype(BF16)
            rhs = jnp.concatenate(
                [yn[n * SGU_CHUNK:(n + 1) * SGU_CHUNK, :] for n in range(n_chunks)], axis=1)
            mixed = _dot(w_mix[g], rhs) + b_sgu_ref[:, g:g + 1]
            for n in range(n_chunks):
                r0 = rows.start + n * SGU_CHUNK
                sgu_scr[r0:r0 + SGU_CHUNK, lo:lo + gd] = (
                    u[n * SGU_CHUNK:(n + 1) * SGU_CHUNK, lo:lo + gd]
                    * mixed[:, n * gd:(n + 1) * gd])
        sgu_ref[0, rows, :] = _rms(sgu_scr[rows, :], g_sgu_out_ref[...]).astype(BF16)


def _in_proj(x, mod, g_pre, w_in, g_ln, w_sgu, b_sgu_t, g_sgu_out, cast_ws, tm, sub):
    bsz, s, d = x.shape
    n_t = s // tm
    n_steps = bsz * n_t
    cast_views = [w.reshape(n_steps, w.shape[0] // n_steps, w.shape[1]) for w in cast_ws]
    cast_specs = [pl.BlockSpec((1,) + w.shape[1:], lambda b, t: (b * n_t + t, 0, 0))
                  for w in cast_views]
    in_w = w_in.shape[1]
    aw = ATTN_WIDTH
    sw = in_w - 3 * aw
    sgu_w = sw // 2
    row_spec = lambda w: pl.BlockSpec((1, tm, w), lambda b, t: (b, t, 0))
    full = lambda shape: pl.BlockSpec(shape, lambda b, t: (0,) * len(shape),
                                      pipeline_mode=pl.Buffered(1))
    n_pairs = aw // LANES
    pair_sds = jax.ShapeDtypeStruct((bsz, n_pairs, s, LANES), BF16)
    pair_spec = pl.BlockSpec((1, n_pairs, tm, LANES), lambda b, t: (b, 0, t, 0))
    outs = pl.pallas_call(
        functools.partial(_in_proj_kernel, sub=sub, n_cast=len(cast_ws)),
        grid=(bsz, n_t),
        in_specs=[
            row_spec(d),
            pl.BlockSpec((1, N_MOD, d), lambda b, t: (b, 0, 0)),
            full((1, d)),
            full((d, in_w)),
            full((1, sgu_w)),
            full((N_SGU_GROUPS, SGU_CHUNK, SGU_CHUNK)),
            full((SGU_CHUNK, N_SGU_GROUPS)),
            full((1, sgu_w)),
        ] + cast_specs,
        out_specs=[pair_spec, pair_spec, pair_spec, row_spec(sgu_w)] + cast_specs,
        out_shape=[pair_sds, pair_sds, pair_sds, jax.ShapeDtypeStruct((bsz, s, sgu_w), BF16)]
        + [jax.ShapeDtypeStruct(w.shape, BF16) for w in cast_views],
        scratch_shapes=[pltpu.VMEM((tm, sgu_w), F32)],
        compiler_params=pltpu.CompilerParams(
            dimension_semantics=("parallel", "parallel"), vmem_limit_bytes=VMEM_LIMIT_BYTES),
        name="in_proj",
    )(x, mod, g_pre, w_in, g_ln, w_sgu, b_sgu_t, g_sgu_out, *cast_views)
    return outs[:4], [o.reshape(w.shape) for o, w in zip(outs[4:], cast_ws)]


def _topk_bias_t(g, n_past):
    nrow = lax.broadcasted_iota(jnp.int32, g.shape, 0)
    beaten = jnp.zeros(g.shape, F32)
    for other in range(n_past):
        b = g[other:other + 1, :]
        beaten = beaten + jnp.where(b > g, 1.0, 0.0)
        beaten = beaten + jnp.where(jnp.logical_and(b == g, nrow > other), 1.0, 0.0)
    return jnp.where(nrow < n_past, jnp.where(beaten < MOBA_TOPK, 0.0, NEG), 0.0)


def _moba_kernel(q_ref, k_ref, v_ref, o_ref, qt_ext, k_ext, vt_ext, shift_scr, *, chunk):
    seq = k_ref.shape[2]
    blk = MOBA_BLOCK
    nb = seq // blk
    half = HEAD_DIM
    lane = lax.broadcasted_iota(jnp.int32, (blk, LANES), 1)
    in_head = (lane < half, lane >= half)
    bias_base = (half, 0)

    trow = lax.broadcasted_iota(jnp.int32, (LANES, blk), 0)
    in_head_t = (trow < half, trow >= half)
    v_rows = (slice(0, half + DEN_ROWS), slice(half - DEN_ROWS, LANES))
    num_rows = (slice(0, half), slice(DEN_ROWS, DEN_ROWS + half))
    den_row = (half, 0)

    nb_rows = -(-nb // 8) * 8
    prow = lax.broadcasted_iota(jnp.int32, (nb_rows, seq), 0)
    pcol = lax.broadcasted_iota(jnp.int32, (nb_rows, seq), 1)
    pblk = lax.shift_right_logical(pcol, blk.bit_length() - 1)
    psel = jnp.where(prow == pblk, 1.0 / blk, 0.0).astype(BF16)
    km_pair = _dot(psel, k_ref[0, 0])
    klane = lax.broadcasted_iota(jnp.int32, (nb_rows, LANES), 1)
    km = (jnp.where(klane < half, km_pair, 0.0).astype(BF16),
          jnp.where(klane >= half, km_pair, 0.0).astype(BF16))

    key_pos = lax.broadcasted_iota(jnp.int32, (blk, blk), 0)
    q_pos = lax.broadcasted_iota(jnp.int32, (blk, blk), 1)
    causal = key_pos <= q_pos

    zeros_half = jnp.zeros((half, blk), F32)
    head_rows = (slice(0, half), slice(half, LANES))
    q_norm2 = ([], [])
    own_logit = ([], [])
    k_norm2_max = [None, None]
    for i in range(nb):
        cols = slice(i * blk, (i + 1) * blk)
        kb = k_ref[0, 0, cols, :].astype(F32)
        vb_t = v_ref[0, 0, cols, :].astype(F32).T
        for hd in range(2):
            onehot = jnp.where(lane == bias_base[hd] + i, 1.0, 0.0)
            k_ext[hd, cols, :] = jnp.where(in_head[hd], kb, onehot).astype(BF16)
            vt_ext[hd, :, cols] = jnp.where(in_head_t[hd], vb_t, 1.0).astype(BF16)

        k_t = kb.T
        q_t = q_ref[0, 0, cols, :].astype(F32).T
        for hd in range(2):
            kh, qh = k_t[head_rows[hd], :], q_t[head_rows[hd], :]
            kn = jnp.sum(kh * kh, axis=0, keepdims=True)
            k_norm2_max[hd] = kn if i == 0 else jnp.maximum(k_norm2_max[hd], kn)
            q_norm2[hd].append(jnp.sum(qh * qh, axis=0, keepdims=True))
            own_logit[hd].append(jnp.sum(qh * kh, axis=0, keepdims=True))

        q_own = (q_t[:half, :], q_t[half:, :])
        for hd in range(2):
            other = zeros_half
            if i > 0:
                pair = [q_own[0], zeros_half] if hd == 0 else [zeros_half, q_own[1]]
                rows = -(-i // 8) * 8
                gate_t = _dot(km[hd][:rows, :], jnp.concatenate(pair, axis=0).astype(BF16))
                other = jnp.concatenate(
                    [_topk_bias_t(gate_t, i), jnp.zeros((half - rows, blk), F32)], axis=0)
            pair = [q_own[0], other] if hd == 0 else [other, q_own[1]]
            qt_ext[hd, :, cols] = jnp.concatenate(pair, axis=0).astype(BF16)

    items = [(i, max(c1 - chunk, 0), c1)
             for i in range(nb) for c1 in range((i + 1) * blk, 0, -chunk)]

    def scores(item):
        i, c0, c1 = item
        return [_dot(k_ext[hd, c0:c1, :], qt_ext[hd, :, i * blk:(i + 1) * blk])
                for hd in range(2)]

    gap = None
    for hd in range(2):
        upper = jnp.sqrt(jnp.concatenate(q_norm2[hd], axis=0)
                         * jnp.max(k_norm2_max[hd], axis=1, keepdims=True))
        shift_scr[hd] = upper - BOUND_HEADROOM
        hd_gap = upper - jnp.concatenate(own_logit[hd], axis=0)
        gap = hd_gap if gap is None else jnp.maximum(gap, hd_gap)
    bounded = jnp.max(gap) <= BOUND_GAP

    def attend(exact_max):
        s_next = scores(items[0])
        m = [None, None]
        acc = [None, None]
        for n, (i, c0, c1) in enumerate(items):
            s_cur = s_next
            if n + 1 < len(items):
                s_next = scores(items[n + 1])
            cw = c1 - c0
            for hd in range(2):
                s = s_cur[hd]
                if c1 == (i + 1) * blk:
                    own = jnp.where(causal, s[cw - blk:, :], NEG)
                    s = own if cw == blk else jnp.concatenate([s[:cw - blk, :], own], axis=0)
                if exact_max:
                    m_chunk = jnp.max(jnp.max(s.reshape(cw // 8, 8, blk), axis=0), axis=0,
                                      keepdims=True)
                    m_new = m_chunk if m[hd] is None else jnp.maximum(m[hd], m_chunk)
                    pv = _dot(vt_ext[hd, v_rows[hd], c0:c1], jnp.exp2(s - m_new).astype(BF16))
                    acc[hd] = pv if acc[hd] is None else acc[hd] * jnp.exp2(m[hd] - m_new) + pv
                    m[hd] = m_new
                else:
                    p = jnp.exp2(s - shift_scr[hd, i:i + 1, :]).astype(BF16)
                    pv = _dot(vt_ext[hd, v_rows[hd], c0:c1], p)
                    acc[hd] = pv if acc[hd] is None else acc[hd] + pv
            if c0 == 0:
                out_t = jnp.concatenate(
                    [acc[hd][num_rows[hd], :] / acc[hd][den_row[hd]:den_row[hd] + 1, :]
                     for hd in range(2)], axis=0)
                o_ref[0, 0, i * blk:(i + 1) * blk, :] = out_t.T.astype(o_ref.dtype)
                m, acc = [None, None], [None, None]

    pl.when(bounded)(functools.partial(attend, False))
    pl.when(jnp.logical_not(bounded))(functools.partial(attend, True))


def _moba(q, k, v, chunk):
    bsz, n_pairs, s, _ = q.shape
    spec = pl.BlockSpec((1, 1, s, LANES), lambda b, p: (b, p, 0, 0))
    return pl.pallas_call(
        functools.partial(_moba_kernel, chunk=chunk),
        grid=(bsz, n_pairs),
        in_specs=[spec, spec, spec],
        out_specs=spec,
        out_shape=jax.ShapeDtypeStruct(q.shape, BF16),
        scratch_shapes=[
            pltpu.VMEM((2, LANES, s), BF16),
            pltpu.VMEM((2, s, LANES), BF16),
            pltpu.VMEM((2, LANES, s), BF16),
            pltpu.VMEM((2, s // MOBA_BLOCK, MOBA_BLOCK), F32),
        ],
        compiler_params=pltpu.CompilerParams(
            dimension_semantics=("parallel", "parallel"), vmem_limit_bytes=VMEM_LIMIT_BYTES),
        name="moba",
    )(q, k, v)


def _mix_ffn_kernel(attn_ref, sgu_ref, x_ref, mod_ref, g_attn_ref, w_out_ref, g_post_mix_ref,
                    g_pre_ref, w_gu_ref, w_down_ref, g_post_ref, o_ref, x_scr, act_scr, *,
                    sub, chunk):
    tm = x_ref.shape[1]
    aw = attn_ref.shape[1] * attn_ref.shape[3]
    d_ff = w_down_ref.shape[0]
    gate_m = mod_ref[0, 2:3, :]
    shift = mod_ref[0, 3:4, :]
    scale = mod_ref[0, 4:5, :]
    gate_f = mod_ref[0, 5:6, :]

    subs = [slice(r0, r0 + sub) for r0 in range(0, tm, sub)]
    hs = []
    for rows in subs:
        attn = jnp.concatenate(
            [attn_ref[0, p, rows, :].astype(F32) for p in range(attn_ref.shape[1])], axis=1)
        a = _rms(attn, g_attn_ref[...]).astype(BF16)
        mix = _dot(a, w_out_ref[0:aw, :]) + _dot(sgu_ref[0, rows, :], w_out_ref[aw:, :])
        x = x_ref[0, rows, :] + gate_m * _rms(mix, g_post_mix_ref[...])
        x_scr[rows, :] = x
        hs.append((_rms(x, g_pre_ref[...]) * (1.0 + scale) + shift).astype(BF16))
    for rows, h in zip(subs, hs):
        for c0 in range(0, d_ff, chunk):
            g = _dot(h, w_gu_ref[:, c0:c0 + chunk])
            up = _dot(h, w_gu_ref[:, d_ff + c0:d_ff + c0 + chunk])
            act_scr[rows, c0:c0 + chunk] = (g * jax.nn.sigmoid(g) * up).astype(BF16)
    for rows in subs:
        y = _dot(act_scr[rows, :], w_down_ref[...])
        o_ref[0, rows, :] = x_scr[rows, :] + gate_f * _rms(y, g_post_ref[...])


def _mix_ffn(attn, sgu, x, mod, g_attn, w_out, g_post_mix, g_pre, w_gu, w_down, g_post, tm, sub,
             chunk):
    bsz, s, d = x.shape
    n_pairs = attn.shape[1]
    aw = n_pairs * attn.shape[3]
    d_ff = w_down.shape[0]
    row_spec = lambda w: pl.BlockSpec((1, tm, w), lambda b, t: (b, t, 0))
    full = lambda shape: pl.BlockSpec(shape, lambda b, t: (0,) * len(shape),
                                      pipeline_mode=pl.Buffered(1))
    return pl.pallas_call(
        functools.partial(_mix_ffn_kernel, sub=sub, chunk=chunk),
        grid=(bsz, s // tm),
        in_specs=[
            pl.BlockSpec((1, n_pairs, tm, attn.shape[3]), lambda b, t: (b, 0, t, 0)),
            row_spec(sgu.shape[2]), row_spec(d),
            pl.BlockSpec((1, N_MOD, d), lambda b, t: (b, 0, 0)),
            full((1, aw)), full(w_out.shape), full((1, d)),
            full((1, d)), full(w_gu.shape), full(w_down.shape), full((1, d)),
        ],
        out_specs=row_spec(d),
        out_shape=jax.ShapeDtypeStruct((bsz, s, d), F32),
        scratch_shapes=[pltpu.VMEM((tm, d), F32), pltpu.VMEM((tm, d_ff), BF16)],
        compiler_params=pltpu.CompilerParams(
            dimension_semantics=("parallel", "parallel"), vmem_limit_bytes=VMEM_LIMIT_BYTES),
        name="mix_ffn",
    )(attn, sgu, x, mod, g_attn, w_out, g_post_mix, g_pre, w_gu, w_down, g_post)


def kernel(x, c, w_ada, b_ada, g_pre_mix, g_post_mix, w_in, g_sgu_norm, w_sgu, b_sgu,
           g_attn_out, g_sgu_out, w_out, g_pre_ffn, g_post_ffn, w_gate_up, w_down):
    bsz, s, d = x.shape
    depth = w_ada.shape[0]
    d_ff = w_down.shape[1]
    assert s % MOBA_BLOCK == 0 and s // MOBA_BLOCK <= HEAD_DIM
    assert w_in.shape[2] == 3 * ATTN_WIDTH + 2 * N_SGU_GROUPS * LANES
    tm_in, sub_in, tm_ffn, sub_ffn = 2048, 256, 1024, 256
    ffn_chunk = 256
    moba_chunk = 2 * MOBA_BLOCK
    assert d_ff % ffn_chunk == 0 and s % tm_in == 0 and s % tm_ffn == 0
    n_in_steps = bsz * (s // tm_in)
    assert d % n_in_steps == 0 and d_ff % n_in_steps == 0
    row = lambda a: a.reshape(1, -1)
    for l in range(depth):
        mod = _ada(c, w_ada[l], b_ada[l]).reshape(bsz, N_MOD, d)
        (q, k, v, sgu), (w_out_b, w_gu_b, w_down_b) = _in_proj(
            x, mod, row(g_pre_mix[l]), w_in[l].astype(BF16), row(g_sgu_norm[l]), w_sgu[l],
            b_sgu[l].T, row(g_sgu_out[l]), [w_out[l], w_gate_up[l], w_down[l]], tm_in, sub_in)
        attn = _moba(q, k, v, moba_chunk)
        x = _mix_ffn(attn, sgu, x, mod, row(g_attn_out[l]), w_out_b, row(g_post_mix[l]),
                     row(g_pre_ffn[l]), w_gu_b, w_down_b, row(g_post_ffn[l]), tm_ffn, sub_ffn,
                     ffn_chunk)
    return x
```

```python
import functools
import math

import jax
import jax.numpy as jnp
from jax import lax
from jax.experimental import pallas as pl
from jax.experimental.pallas import tpu as pltpu

N_HEADS = 8
HEAD_DIM = 64
ATTN_WIDTH = N_HEADS * HEAD_DIM
N_SGU_GROUPS = 4
SGU_CHUNK = 128
MOBA_BLOCK = 256
MOBA_TOPK = 3
N_MOD = 6
EPS = 1e-6
NEG = -1e30

LANES = 128
DEN_ROWS = 16
BOUND_HEADROOM = 60.0
BOUND_GAP = 160.0
VMEM_LIMIT_BYTES = 56 * 1024 * 1024
Q_SCALE = HEAD_DIM ** -0.5 * math.log2(math.e)

BF16 = jnp.bfloat16
F32 = jnp.float32


def _rms(x, g):
    return x * lax.rsqrt(jnp.mean(x * x, axis=-1, keepdims=True) + EPS) * g


def _dot(a, b):
    return jnp.dot(a, b, preferred_element_type=F32)


def _ada_kernel(c_ref, w_ref, b_ref, o_ref):
    c = c_ref[...]
    c_act = (c * jax.nn.sigmoid(c)).astype(BF16)
    o_ref[...] = _dot(c_act, w_ref[...].astype(BF16)) + b_ref[...]


def _ada(c, w_ada, b_ada):
    bsz, d = c.shape
    n = w_ada.shape[1]
    tn = d
    return pl.pallas_call(
        _ada_kernel,
        grid=(n // tn,),
        in_specs=[
            pl.BlockSpec((bsz, d), lambda j: (0, 0)),
            pl.BlockSpec((d, tn), lambda j: (0, j)),
            pl.BlockSpec((1, tn), lambda j: (0, j)),
        ],
        out_specs=pl.BlockSpec((bsz, tn), lambda j: (0, j)),
        out_shape=jax.ShapeDtypeStruct((bsz, n), F32),
        compiler_params=pltpu.CompilerParams(
            dimension_semantics=("arbitrary",), vmem_limit_bytes=VMEM_LIMIT_BYTES),
        name="ada",
    )(c, w_ada, b_ada.reshape(1, n))


def _in_proj_kernel(*refs, sub, n_cast):
    (x_ref, mod_ref, g_pre_ref, w_in_ref, g_ln_ref, w_sgu_ref, b_sgu_ref,
     g_sgu_out_ref) = refs[:8]
    cast_in = refs[8:8 + n_cast]
    q_ref, k_ref, v_ref, sgu_ref = refs[8 + n_cast:12 + n_cast]
    cast_out = refs[12 + n_cast:12 + 2 * n_cast]
    sgu_scr = refs[12 + 2 * n_cast]
    for w_ref, o_ref in zip(cast_in, cast_out):
        o_ref[...] = w_ref[...].astype(BF16)

    tm = x_ref.shape[1]
    aw = ATTN_WIDTH
    gd = LANES
    n_chunks = sub // SGU_CHUNK
    shift = mod_ref[0, 0:1, :]
    scale = mod_ref[0, 1:2, :]
    row = lax.broadcasted_iota(jnp.int32, (SGU_CHUNK, SGU_CHUNK), 0)
    col = lax.broadcasted_iota(jnp.int32, (SGU_CHUNK, SGU_CHUNK), 1)
    w_mix = [jnp.where(col <= row, w_sgu_ref[g], 0.0).astype(BF16) for g in range(N_SGU_GROUPS)]

    subs = [slice(r0, r0 + sub) for r0 in range(0, tm, sub)]
    hs = [(_rms(x_ref[0, rows, :], g_pre_ref[...]) * (1.0 + scale) + shift).astype(BF16)
          for rows in subs]
    projs = [_dot(h, w_in_ref[...]) for h in hs]
    for rows, proj in zip(subs, projs):
        for p in range(aw // LANES):
            lanes = slice(p * LANES, (p + 1) * LANES)
            q_ref[0, p, rows, :] = (proj[:, 0:aw][:, lanes] * Q_SCALE).astype(BF16)
            k_ref[0, p, rows, :] = proj[:, aw:2 * aw][:, lanes].astype(BF16)
            v_ref[0, p, rows, :] = proj[:, 2 * aw:3 * aw][:, lanes].astype(BF16)

        u = jax.nn.gelu(proj[:, 3 * aw:3 * aw + N_SGU_GROUPS * gd])
        vg = jax.nn.gelu(proj[:, 3 * aw + N_SGU_GROUPS * gd:])
        for g in range(N_SGU_GROUPS):
            lo = g * gd
            vgg = vg[:, lo:lo + gd]
            d = vgg - jnp.mean(vgg, axis=-1, keepdims=True)
            yn = (d * lax.rsqrt(jnp.mean(d * d, axis=-1, keepdims=True) + EPS)
                  * g_ln_ref[:, lo:lo + gd]).astype(BF16)
            rhs = jnp.concatenate(
                [yn[n * SGU_CHUNK:(n + 1) * SGU_CHUNK, :] for n in range(n_chunks)], axis=1)
            mixed = _dot(w_mix[g], rhs) + b_sgu_ref[:, g:g + 1]
            for n in range(n_chunks):
                r0 = rows.start + n * SGU_CHUNK
                sgu_scr[r0:r0 + SGU_CHUNK, lo:lo + gd] = (
                    u[n * SGU_CHUNK:(n + 1) * SGU_CHUNK, lo:lo + gd]
                    * mixed[:, n * gd:(n + 1) * gd])
        sgu_ref[0, rows, :] = _rms(sgu_scr[rows, :], g_sgu_out_ref[...]).astype(BF16)


def _in_proj(x, mod, g_pre, w_in, g_ln, w_sgu, b_sgu_t, g_sgu_out, cast_ws, tm, sub):
    bsz, s, d = x.shape
    n_t = s // tm
    n_steps = bsz * n_t
    cast_views = [w.reshape(n_steps, w.shape[0] // n_steps, w.shape[1]) for w in cast_ws]
    cast_specs = [pl.BlockSpec((1,) + w.shape[1:], lambda b, t: (b * n_t + t, 0, 0))
                  for w in cast_views]
    in_w = w_in.shape[1]
    aw = ATTN_WIDTH
    sw = in_w - 3 * aw
    sgu_w = sw // 2
    row_spec = lambda w: pl.BlockSpec((1, tm, w), lambda b, t: (b, t, 0))
    full = lambda shape: pl.BlockSpec(shape, lambda b, t: (0,) * len(shape),
                                      pipeline_mode=pl.Buffered(1))
    n_pairs = aw // LANES
    pair_sds = jax.ShapeDtypeStruct((bsz, n_pairs, s, LANES), BF16)
    pair_spec = pl.BlockSpec((1, n_pairs, tm, LANES), lambda b, t: (b, 0, t, 0))
    outs = pl.pallas_call(
        functools.partial(_in_proj_kernel, sub=sub, n_cast=len(cast_ws)),
        grid=(bsz, n_t),
        in_specs=[
            row_spec(d),
            pl.BlockSpec((1, N_MOD, d), lambda b, t: (b, 0, 0)),
            full((1, d)),
            full((d, in_w)),
            full((1, sgu_w)),
            full((N_SGU_GROUPS, SGU_CHUNK, SGU_CHUNK)),
            full((SGU_CHUNK, N_SGU_GROUPS)),
            full((1, sgu_w)),
        ] + cast_specs,
        out_specs=[pair_spec, pair_spec, pair_spec, row_spec(sgu_w)] + cast_specs,
        out_shape=[pair_sds, pair_sds, pair_sds, jax.ShapeDtypeStruct((bsz, s, sgu_w), BF16)]
        + [jax.ShapeDtypeStruct(w.shape, BF16) for w in cast_views],
        scratch_shapes=[pltpu.VMEM((tm, sgu_w), F32)],
        compiler_params=pltpu.CompilerParams(
            dimension_semantics=("parallel", "parallel"), vmem_limit_bytes=VMEM_LIMIT_BYTES),
        name="in_proj",
    )(x, mod, g_pre, w_in, g_ln, w_sgu, b_sgu_t, g_sgu_out, *cast_views)
    return outs[:4], [o.reshape(w.shape) for o, w in zip(outs[4:], cast_ws)]


def _topk_bias_t(g, n_past):
    nrow = lax.broadcasted_iota(jnp.int32, g.shape, 0)
    beaten = jnp.zeros(g.shape, F32)
    for other in range(n_past):
        b = g[other:other + 1, :]
        beaten = beaten + jnp.where(b > g, 1.0, 0.0)
        beaten = beaten + jnp.where(jnp.logical_and(b == g, nrow > other), 1.0, 0.0)
    return jnp.where(nrow < n_past, jnp.where(beaten < MOBA_TOPK, 0.0, NEG), 0.0)


def _moba_kernel(q_ref, k_ref, v_ref, o_ref, qt_ext, k_ext, vt_ext, shift_scr, *, chunk):
    seq = k_ref.shape[2]
    blk = MOBA_BLOCK
    nb = seq // blk
    half = HEAD_DIM
    lane = lax.broadcasted_iota(jnp.int32, (blk, LANES), 1)
    in_head = (lane < half, lane >= half)
    bias_base = (half, 0)

    trow = lax.broadcasted_iota(jnp.int32, (LANES, blk), 0)
    in_head_t = (trow < half, trow >= half)
    v_rows = (slice(0, half + DEN_ROWS), slice(half - DEN_ROWS, LANES))
    num_rows = (slice(0, half), slice(DEN_ROWS, DEN_ROWS + half))
    den_row = (half, 0)

    key_pos = lax.broadcasted_iota(jnp.int32, (blk, blk), 0)
    q_pos = lax.broadcasted_iota(jnp.int32, (blk, blk), 1)
    causal = key_pos <= q_pos

    head_rows = (slice(0, half), slice(half, LANES))
    other_rows = (slice(half, LANES), slice(0, half))
    q_norm2 = ([], [])
    own_logit = ([], [])
    k_norm2_max = [None, None]
    for i in range(nb):
        cols = slice(i * blk, (i + 1) * blk)
        k_t = k_ref[0, 0, cols, :].astype(F32).T
        q_t = q_ref[0, 0, cols, :].astype(F32).T
        for hd in range(2):
            kh, qh = k_t[head_rows[hd], :], q_t[head_rows[hd], :]
            kn = jnp.sum(kh * kh, axis=0, keepdims=True)
            k_norm2_max[hd] = kn if i == 0 else jnp.maximum(k_norm2_max[hd], kn)
            q_norm2[hd].append(jnp.sum(qh * qh, axis=0, keepdims=True))
            own_logit[hd].append(jnp.sum(qh * kh, axis=0, keepdims=True))
            qt_ext[hd, head_rows[hd], cols] = qh.astype(BF16)

    def prepare_operands():
        nb_rows = -(-nb // 8) * 8
        prow = lax.broadcasted_iota(jnp.int32, (nb_rows, seq), 0)
        pcol = lax.broadcasted_iota(jnp.int32, (nb_rows, seq), 1)
        pblk = lax.shift_right_logical(pcol, blk.bit_length() - 1)
        psel = jnp.where(prow == pblk, 1.0 / blk, 0.0).astype(BF16)
        km_pair = _dot(psel, k_ref[0, 0])
        klane = lax.broadcasted_iota(jnp.int32, (nb_rows, LANES), 1)
        km = (jnp.where(klane < half, km_pair, 0.0).astype(BF16),
              jnp.where(klane >= half, km_pair, 0.0).astype(BF16))
        zeros_half = jnp.zeros((half, blk), BF16)
        for i in range(nb):
            cols = slice(i * blk, (i + 1) * blk)
            kb = k_ref[0, 0, cols, :].astype(F32)
            vb_t = v_ref[0, 0, cols, :].astype(F32).T
            for hd in range(2):
                onehot = jnp.where(lane == bias_base[hd] + i, 1.0, 0.0)
                k_ext[hd, cols, :] = jnp.where(in_head[hd], kb, onehot).astype(BF16)
                vt_ext[hd, :, cols] = jnp.where(in_head_t[hd], vb_t, 1.0).astype(BF16)
                bias = jnp.zeros((half, blk), F32)
                if i > 0:
                    q_own = qt_ext[hd, head_rows[hd], cols]
                    pair = [q_own, zeros_half] if hd == 0 else [zeros_half, q_own]
                    rows = -(-i // 8) * 8
                    gate_t = _dot(km[hd][:rows, :], jnp.concatenate(pair, axis=0))
                    bias = jnp.concatenate(
                        [_topk_bias_t(gate_t, i), jnp.zeros((half - rows, blk), F32)], axis=0)
                qt_ext[hd, other_rows[hd], cols] = bias.astype(BF16)

    items = [(i, max(c1 - chunk, 0), c1)
             for i in range(nb) for c1 in range((i + 1) * blk, 0, -chunk)]

    def scores(item):
        i, c0, c1 = item
        return [_dot(k_ext[hd, c0:c1, :], qt_ext[hd, :, i * blk:(i + 1) * blk])
                for hd in range(2)]

    gap = None
    for hd in range(2):
        upper = jnp.sqrt(jnp.concatenate(q_norm2[hd], axis=0)
                         * jnp.max(k_norm2_max[hd], axis=1, keepdims=True))
        shift_scr[hd] = upper - BOUND_HEADROOM
        hd_gap = upper - jnp.concatenate(own_logit[hd], axis=0)
        gap = hd_gap if gap is None else jnp.maximum(gap, hd_gap)
    bounded = jnp.max(gap) <= BOUND_GAP

    def attend(exact_max):
        prepare_operands()
        s_next = scores(items[0])
        m = [None, None]
        acc = [None, None]
        for n, (i, c0, c1) in enumerate(items):
            s_cur = s_next
            if n + 1 < len(items):
                s_next = scores(items[n + 1])
            cw = c1 - c0
            for hd in range(2):
                s = s_cur[hd]
                if c1 == (i + 1) * blk:
                    own = jnp.where(causal, s[cw - blk:, :], NEG)
                    s = own if cw == blk else jnp.concatenate([s[:cw - blk, :], own], axis=0)
                if exact_max:
                    m_chunk = jnp.max(jnp.max(s.reshape(cw // 8, 8, blk), axis=0), axis=0,
                                      keepdims=True)
                    m_new = m_chunk if m[hd] is None else jnp.maximum(m[hd], m_chunk)
                    pv = _dot(vt_ext[hd, v_rows[hd], c0:c1], jnp.exp2(s - m_new).astype(BF16))
                    acc[hd] = pv if acc[hd] is None else acc[hd] * jnp.exp2(m[hd] - m_new) + pv
                    m[hd] = m_new
                else:
                    p = jnp.exp2(s - shift_scr[hd, i:i + 1, :]).astype(BF16)
                    pv = _dot(vt_ext[hd, v_rows[hd], c0:c1], p)
                    acc[hd] = pv if acc[hd] is None else acc[hd] + pv
            if c0 == 0:
                out_t = jnp.concatenate(
                    [acc[hd][num_rows[hd], :] / acc[hd][den_row[hd]:den_row[hd] + 1, :]
                     for hd in range(2)], axis=0)
                o_ref[0, 0, i * blk:(i + 1) * blk, :] = out_t.T.astype(o_ref.dtype)
                m, acc = [None, None], [None, None]

    pl.when(bounded)(functools.partial(attend, False))
    pl.when(jnp.logical_not(bounded))(functools.partial(attend, True))


def _moba(q, k, v, chunk):
    bsz, n_pairs, s, _ = q.shape
    spec = pl.BlockSpec((1, 1, s, LANES), lambda b, p: (b, p, 0, 0))
    return pl.pallas_call(
        functools.partial(_moba_kernel, chunk=chunk),
        grid=(bsz, n_pairs),
        in_specs=[spec, spec, spec],
        out_specs=spec,
        out_shape=jax.ShapeDtypeStruct(q.shape, BF16),
        scratch_shapes=[
            pltpu.VMEM((2, LANES, s), BF16),
            pltpu.VMEM((2, s, LANES), BF16),
            pltpu.VMEM((2, LANES, s), BF16),
            pltpu.VMEM((2, s // MOBA_BLOCK, MOBA_BLOCK), F32),
        ],
        compiler_params=pltpu.CompilerParams(
            dimension_semantics=("parallel", "parallel"), vmem_limit_bytes=VMEM_LIMIT_BYTES),
        name="moba",
    )(q, k, v)


def _mix_ffn_kernel(attn_ref, sgu_ref, x_ref, mod_ref, g_attn_ref, w_out_ref, g_post_mix_ref,
                    g_pre_ref, w_gu_ref, w_down_ref, g_post_ref, o_ref, x_scr, act_scr, *,
                    sub, chunk):
    tm = x_ref.shape[1]
    aw = attn_ref.shape[1] * attn_ref.shape[3]
    d_ff = w_down_ref.shape[0]
    gate_m = mod_ref[0, 2:3, :]
    shift = mod_ref[0, 3:4, :]
    scale = mod_ref[0, 4:5, :]
    gate_f = mod_ref[0, 5:6, :]

    subs = [slice(r0, r0 + sub) for r0 in range(0, tm, sub)]
    hs = []
    for rows in subs:
        attn = jnp.concatenate(
            [attn_ref[0, p, rows, :].astype(F32) for p in range(attn_ref.shape[1])], axis=1)
        a = _rms(attn, g_attn_ref[...]).astype(BF16)
        mix = _dot(a, w_out_ref[0:aw, :]) + _dot(sgu_ref[0, rows, :], w_out_ref[aw:, :])
        x = x_ref[0, rows, :] + gate_m * _rms(mix, g_post_mix_ref[...])
        x_scr[rows, :] = x
        hs.append((_rms(x, g_pre_ref[...]) * (1.0 + scale) + shift).astype(BF16))
    for rows, h in zip(subs, hs):
        for c0 in range(0, d_ff, chunk):
            g = _dot(h, w_gu_ref[:, c0:c0 + chunk])
            up = _dot(h, w_gu_ref[:, d_ff + c0:d_ff + c0 + chunk])
            act_scr[rows, c0:c0 + chunk] = (g * jax.nn.sigmoid(g) * up).astype(BF16)
    for rows in subs:
        y = _dot(act_scr[rows, :], w_down_ref[...])
        o_ref[0, rows, :] = x_scr[rows, :] + gate_f * _rms(y, g_post_ref[...])


def _mix_ffn(attn, sgu, x, mod, g_attn, w_out, g_post_mix, g_pre, w_gu, w_down, g_post, tm, sub,
             chunk):
    bsz, s, d = x.shape
    n_pairs = attn.shape[1]
    aw = n_pairs * attn.shape[3]
    d_ff = w_down.shape[0]
    row_spec = lambda w: pl.BlockSpec((1, tm, w), lambda b, t: (b, t, 0))
    full = lambda shape: pl.BlockSpec(shape, lambda b, t: (0,) * len(shape),
                                      pipeline_mode=pl.Buffered(1))
    return pl.pallas_call(
        functools.partial(_mix_ffn_kernel, sub=sub, chunk=chunk),
        grid=(bsz, s // tm),
        in_specs=[
            pl.BlockSpec((1, n_pairs, tm, attn.shape[3]), lambda b, t: (b, 0, t, 0)),
            row_spec(sgu.shape[2]), row_spec(d),
            pl.BlockSpec((1, N_MOD, d), lambda b, t: (b, 0, 0)),
            full((1, aw)), full(w_out.shape), full((1, d)),
            full((1, d)), full(w_gu.shape), full(w_down.shape), full((1, d)),
        ],
        out_specs=row_spec(d),
        out_shape=jax.ShapeDtypeStruct((bsz, s, d), F32),
        scratch_shapes=[pltpu.VMEM((tm, d), F32), pltpu.VMEM((tm, d_ff), BF16)],
        compiler_params=pltpu.CompilerParams(
            dimension_semantics=("parallel", "parallel"), vmem_limit_bytes=VMEM_LIMIT_BYTES),
        name="mix_ffn",
    )(attn, sgu, x, mod, g_attn, w_out, g_post_mix, g_pre, w_gu, w_down, g_post)


def kernel(x, c, w_ada, b_ada, g_pre_mix, g_post_mix, w_in, g_sgu_norm, w_sgu, b_sgu,
           g_attn_out, g_sgu_out, w_out, g_pre_ffn, g_post_ffn, w_gate_up, w_down):
    bsz, s, d = x.shape
    depth = w_ada.shape[0]
    d_ff = w_down.shape[1]
    assert s % MOBA_BLOCK == 0 and s // MOBA_BLOCK <= HEAD_DIM
    assert w_in.shape[2] == 3 * ATTN_WIDTH + 2 * N_SGU_GROUPS * LANES
    tm_in, sub_in, tm_ffn, sub_ffn = 2048, 256, 1024, 256
    ffn_chunk = 256
    moba_chunk = 2 * MOBA_BLOCK
    assert d_ff % ffn_chunk == 0 and s % tm_in == 0 and s % tm_ffn == 0
    n_in_steps = bsz * (s // tm_in)
    assert d % n_in_steps == 0 and d_ff % n_in_steps == 0
    row = lambda a: a.reshape(1, -1)
    for l in range(depth):
        mod = _ada(c, w_ada[l], b_ada[l]).reshape(bsz, N_MOD, d)
        (q, k, v, sgu), (w_out_b, w_gu_b, w_down_b) = _in_proj(
            x, mod, row(g_pre_mix[l]), w_in[l].astype(BF16), row(g_sgu_norm[l]), w_sgu[l],
            b_sgu[l].T, row(g_sgu_out[l]), [w_out[l], w_gate_up[l], w_down[l]], tm_in, sub_in)
        attn = _moba(q, k, v, moba_chunk)
        x = _mix_ffn(attn, sgu, x, mod, row(g_attn_out[l]), w_out_b, row(g_post_mix[l]),
                     row(g_pre_ffn[l]), w_gu_b, w_down_b, row(g_post_ffn[l]), tm_ffn, sub_ffn,
                     ffn_chunk)
    return x
```

```python
import functools
import math

import jax
import jax.numpy as jnp
from jax import lax
from jax.experimental import pallas as pl
from jax.experimental.pallas import tpu as pltpu

N_HEADS = 8
HEAD_DIM = 64
ATTN_WIDTH = N_HEADS * HEAD_DIM
N_SGU_GROUPS = 4
SGU_CHUNK = 128
MOBA_BLOCK = 256
MOBA_TOPK = 3
N_MOD = 6
EPS = 1e-6
NEG = -1e30

LANES = 128
DEN_ROWS = 16
BOUND_HEADROOM = 80.0
BOUND_GAP = 180.0
V_LIMIT = 2.0 ** 30
VMEM_LIMIT_BYTES = 56 * 1024 * 1024
Q_SCALE = HEAD_DIM ** -0.5 * math.log2(math.e)

BF16 = jnp.bfloat16
F32 = jnp.float32


def _rms(x, g):
    return x * lax.rsqrt(jnp.mean(x * x, axis=-1, keepdims=True) + EPS) * g


def _dot(a, b):
    return jnp.dot(a, b, preferred_element_type=F32)


def _ada_kernel(c_ref, w_ref, b_ref, o_ref):
    c = c_ref[...]
    c_act = (c * jax.nn.sigmoid(c)).astype(BF16)
    o_ref[...] = _dot(c_act, w_ref[...].astype(BF16)) + b_ref[...]


def _ada(c, w_ada, b_ada):
    bsz, d = c.shape
    n = w_ada.shape[1]
    tn = d
    return pl.pallas_call(
        _ada_kernel,
        grid=(n // tn,),
        in_specs=[
            pl.BlockSpec((bsz, d), lambda j: (0, 0)),
            pl.BlockSpec((d, tn), lambda j: (0, j)),
            pl.BlockSpec((1, tn), lambda j: (0, j)),
        ],
        out_specs=pl.BlockSpec((bsz, tn), lambda j: (0, j)),
        out_shape=jax.ShapeDtypeStruct((bsz, n), F32),
        compiler_params=pltpu.CompilerParams(
            dimension_semantics=("arbitrary",), vmem_limit_bytes=VMEM_LIMIT_BYTES),
        name="ada",
    )(c, w_ada, b_ada.reshape(1, n))


def _in_proj_kernel(*refs, sub, n_cast):
    (x_ref, mod_ref, g_pre_ref, w_in_ref, g_ln_ref, w_sgu_ref, b_sgu_ref,
     g_sgu_out_ref) = refs[:8]
    cast_in = refs[8:8 + n_cast]
    q_ref, k_ref, v_ref, sgu_ref = refs[8 + n_cast:12 + n_cast]
    cast_out = refs[12 + n_cast:12 + 2 * n_cast]
    sgu_scr = refs[12 + 2 * n_cast]
    for w_ref, o_ref in zip(cast_in, cast_out):
        o_ref[...] = w_ref[...].astype(BF16)

    tm = x_ref.shape[1]
    aw = ATTN_WIDTH
    gd = LANES
    n_chunks = sub // SGU_CHUNK
    shift = mod_ref[0, 0:1, :]
    scale = mod_ref[0, 1:2, :]
    row = lax.broadcasted_iota(jnp.int32, (SGU_CHUNK, SGU_CHUNK), 0)
    col = lax.broadcasted_iota(jnp.int32, (SGU_CHUNK, SGU_CHUNK), 1)
    w_mix = [jnp.where(col <= row, w_sgu_ref[g], 0.0).astype(BF16) for g in range(N_SGU_GROUPS)]

    subs = [slice(r0, r0 + sub) for r0 in range(0, tm, sub)]
    hs = [(_rms(x_ref[0, rows, :], g_pre_ref[...]) * (1.0 + scale) + shift).astype(BF16)
          for rows in subs]
    projs = [_dot(h, w_in_ref[...]) for h in hs]
    for rows, proj in zip(subs, projs):
        for p in range(aw // LANES):
            lanes = slice(p * LANES, (p + 1) * LANES)
            q_ref[0, p, rows, :] = (proj[:, 0:aw][:, lanes] * Q_SCALE).astype(BF16)
            k_ref[0, p, rows, :] = proj[:, aw:2 * aw][:, lanes].astype(BF16)
            v_ref[0, p, rows, :] = proj[:, 2 * aw:3 * aw][:, lanes].astype(BF16)

        u = jax.nn.gelu(proj[:, 3 * aw:3 * aw + N_SGU_GROUPS * gd])
        vg = jax.nn.gelu(proj[:, 3 * aw + N_SGU_GROUPS * gd:])
        for g in range(N_SGU_GROUPS):
            lo = g * gd
            vgg = vg[:, lo:lo + gd]
            d = vgg - jnp.mean(vgg, axis=-1, keepdims=True)
            yn = (d * lax.rsqrt(jnp.mean(d * d, axis=-1, keepdims=True) + EPS)
                  * g_ln_ref[:, lo:lo + gd]).astype(BF16)
            rhs = jnp.concatenate(
                [yn[n * SGU_CHUNK:(n + 1) * SGU_CHUNK, :] for n in range(n_chunks)], axis=1)
            mixed = _dot(w_mix[g], rhs) + b_sgu_ref[:, g:g + 1]
            for n in range(n_chunks):
                r0 = rows.start + n * SGU_CHUNK
                sgu_scr[r0:r0 + SGU_CHUNK, lo:lo + gd] = (
                    u[n * SGU_CHUNK:(n + 1) * SGU_CHUNK, lo:lo + gd]
                    * mixed[:, n * gd:(n + 1) * gd])
        sgu_ref[0, rows, :] = _rms(sgu_scr[rows, :], g_sgu_out_ref[...]).astype(BF16)


def _in_proj(x, mod, g_pre, w_in, g_ln, w_sgu, b_sgu_t, g_sgu_out, cast_ws, tm, sub):
    bsz, s, d = x.shape
    n_t = s // tm
    n_steps = bsz * n_t
    cast_views = [w.reshape(n_steps, w.shape[0] // n_steps, w.shape[1]) for w in cast_ws]
    cast_specs = [pl.BlockSpec((1,) + w.shape[1:], lambda b, t: (b * n_t + t, 0, 0))
                  for w in cast_views]
    in_w = w_in.shape[1]
    aw = ATTN_WIDTH
    sw = in_w - 3 * aw
    sgu_w = sw // 2
    row_spec = lambda w: pl.BlockSpec((1, tm, w), lambda b, t: (b, t, 0))
    full = lambda shape: pl.BlockSpec(shape, lambda b, t: (0,) * len(shape),
                                      pipeline_mode=pl.Buffered(1))
    n_pairs = aw // LANES
    pair_sds = jax.ShapeDtypeStruct((bsz, n_pairs, s, LANES), BF16)
    pair_spec = pl.BlockSpec((1, n_pairs, tm, LANES), lambda b, t: (b, 0, t, 0))
    outs = pl.pallas_call(
        functools.partial(_in_proj_kernel, sub=sub, n_cast=len(cast_ws)),
        grid=(bsz, n_t),
        in_specs=[
            row_spec(d),
            pl.BlockSpec((1, N_MOD, d), lambda b, t: (b, 0, 0)),
            full((1, d)),
            full((d, in_w)),
            full((1, sgu_w)),
            full((N_SGU_GROUPS, SGU_CHUNK, SGU_CHUNK)),
            full((SGU_CHUNK, N_SGU_GROUPS)),
            full((1, sgu_w)),
        ] + cast_specs,
        out_specs=[pair_spec, pair_spec, pair_spec, row_spec(sgu_w)] + cast_specs,
        out_shape=[pair_sds, pair_sds, pair_sds, jax.ShapeDtypeStruct((bsz, s, sgu_w), BF16)]
        + [jax.ShapeDtypeStruct(w.shape, BF16) for w in cast_views],
        scratch_shapes=[pltpu.VMEM((tm, sgu_w), F32)],
        compiler_params=pltpu.CompilerParams(
            dimension_semantics=("parallel", "parallel"), vmem_limit_bytes=VMEM_LIMIT_BYTES),
        name="in_proj",
    )(x, mod, g_pre, w_in, g_ln, w_sgu, b_sgu_t, g_sgu_out, *cast_views)
    return outs[:4], [o.reshape(w.shape) for o, w in zip(outs[4:], cast_ws)]


def _topk_bias_t(g, n_past):
    nrow = lax.broadcasted_iota(jnp.int32, g.shape, 0)
    beaten = jnp.zeros(g.shape, F32)
    for other in range(n_past):
        b = g[other:other + 1, :]
        beaten = beaten + jnp.where(b > g, 1.0, 0.0)
        beaten = beaten + jnp.where(jnp.logical_and(b == g, nrow > other), 1.0, 0.0)
    return jnp.where(nrow < n_past, jnp.where(beaten < MOBA_TOPK, 0.0, NEG), 0.0)


def _moba_kernel(q_ref, k_ref, v_ref, o_ref, qt_ext, k_ext, vt_ext, shift_scr, *, chunk):
    seq = k_ref.shape[2]
    blk = MOBA_BLOCK
    nb = seq // blk
    half = HEAD_DIM
    lane = lax.broadcasted_iota(jnp.int32, (blk, LANES), 1)
    in_head = (lane < half, lane >= half)
    bias_base = (half, 0)

    trow = lax.broadcasted_iota(jnp.int32, (LANES, blk), 0)
    in_head_t = (trow < half, trow >= half)
    v_rows = (slice(0, half + DEN_ROWS), slice(half - DEN_ROWS, LANES))
    num_rows = (slice(0, half), slice(DEN_ROWS, DEN_ROWS + half))
    den_row = (half, 0)

    key_pos = lax.broadcasted_iota(jnp.int32, (blk, blk), 0)
    q_pos = lax.broadcasted_iota(jnp.int32, (blk, blk), 1)
    causal = key_pos <= q_pos

    head_rows = (slice(0, half), slice(half, LANES))
    other_rows = (slice(half, LANES), slice(0, half))
    zeros_half_t = jnp.zeros((half, blk), BF16)
    q_norm2 = ([], [])
    first_logit = ([], [])
    k_bound2 = [None, None]
    chan_lane = lax.broadcasted_iota(jnp.int32, (1, LANES), 1)
    chan_in_head = (chan_lane < half, chan_lane >= half)
    for i in range(nb):
        cols = slice(i * blk, (i + 1) * blk)
        kb = k_ref[0, 0, cols, :].astype(F32)
        chan_max2 = jnp.max(kb * kb, axis=0, keepdims=True)
        for hd in range(2):
            blk_bound2 = jnp.sum(jnp.where(chan_in_head[hd], chan_max2, 0.0), axis=1,
                                 keepdims=True)
            k_bound2[hd] = blk_bound2 if i == 0 else jnp.maximum(k_bound2[hd], blk_bound2)
        v_abs = jnp.max(jnp.abs(v_ref[0, 0, cols, :]), axis=0, keepdims=True).astype(F32)
        v_abs_max = v_abs if i == 0 else jnp.maximum(v_abs_max, v_abs)
        k_first = k_ref[0, 0, i * blk:i * blk + 16, :]
        q_t = q_ref[0, 0, cols, :].astype(F32).T
        for hd in range(2):
            qh = q_t[head_rows[hd], :]
            qh_b = qh.astype(BF16)
            pair = [qh_b, zeros_half_t] if hd == 0 else [zeros_half_t, qh_b]
            q_norm2[hd].append(jnp.sum(qh * qh, axis=0, keepdims=True))
            first_logit[hd].append(_dot(k_first, jnp.concatenate(pair, axis=0))[0:1, :])
            qt_ext[hd, head_rows[hd], cols] = qh_b

    def prepare_operands():
        zeros_half = jnp.zeros((half, blk), BF16)
        k_means = []
        for i in range(nb):
            cols = slice(i * blk, (i + 1) * blk)
            kb = k_ref[0, 0, cols, :].astype(F32)
            vb_t = v_ref[0, 0, cols, :].astype(F32).T
            if i > 0:
                rows = -(-i // 8) * 8
                km_pair = jnp.concatenate(
                    k_means + [jnp.zeros((rows - i, LANES), F32)] * (rows > i), axis=0)
                klane = lax.broadcasted_iota(jnp.int32, (rows, LANES), 1)
                km = (jnp.where(klane < half, km_pair, 0.0).astype(BF16),
                      jnp.where(klane >= half, km_pair, 0.0).astype(BF16))
            k_means.append(jnp.sum(kb, axis=0, keepdims=True) * (1.0 / blk))
            for hd in range(2):
                onehot = jnp.where(lane == bias_base[hd] + i, 1.0, 0.0)
                k_ext[hd, cols, :] = jnp.where(in_head[hd], kb, onehot).astype(BF16)
                vt_ext[hd, :, cols] = jnp.where(in_head_t[hd], vb_t, 1.0).astype(BF16)
                bias = jnp.zeros((half, blk), F32)
                if i > 0:
                    q_own = qt_ext[hd, head_rows[hd], cols]
                    pair = [q_own, zeros_half] if hd == 0 else [zeros_half, q_own]
                    gate_t = _dot(km[hd], jnp.concatenate(pair, axis=0))
                    bias = jnp.concatenate(
                        [_topk_bias_t(gate_t, i), jnp.zeros((half - rows, blk), F32)], axis=0)
                qt_ext[hd, other_rows[hd], cols] = bias.astype(BF16)

    items = [(i, max(c1 - chunk, 0), c1)
             for i in range(nb) for c1 in range((i + 1) * blk, 0, -chunk)]

    def scores(item):
        i, c0, c1 = item
        return [_dot(k_ext[hd, c0:c1, :], qt_ext[hd, :, i * blk:(i + 1) * blk])
                for hd in range(2)]

    gap = None
    for hd in range(2):
        upper = jnp.sqrt(jnp.concatenate(q_norm2[hd], axis=0) * k_bound2[hd])
        shift_scr[hd] = upper - BOUND_HEADROOM
        hd_gap = upper - jnp.concatenate(first_logit[hd], axis=0)
        gap = hd_gap if gap is None else jnp.maximum(gap, hd_gap)
    bounded = jnp.logical_and(jnp.max(gap) <= BOUND_GAP, jnp.max(v_abs_max) <= V_LIMIT)

    def attend(exact_max):
        prepare_operands()
        s_next = scores(items[0])
        m = [None, None]
        acc = [None, None]
        for n, (i, c0, c1) in enumerate(items):
            s_cur = s_next
            if n + 1 < len(items):
                s_next = scores(items[n + 1])
            cw = c1 - c0
            for hd in range(2):
                s = s_cur[hd]
                if c1 == (i + 1) * blk:
                    own = jnp.where(causal, s[cw - blk:, :], NEG)
                    s = own if cw == blk else jnp.concatenate([s[:cw - blk, :], own], axis=0)
                if exact_max:
                    m_chunk = jnp.max(jnp.max(s.reshape(cw // 8, 8, blk), axis=0), axis=0,
                                      keepdims=True)
                    m_new = m_chunk if m[hd] is None else jnp.maximum(m[hd], m_chunk)
                    pv = _dot(vt_ext[hd, v_rows[hd], c0:c1], jnp.exp2(s - m_new).astype(BF16))
                    acc[hd] = pv if acc[hd] is None else acc[hd] * jnp.exp2(m[hd] - m_new) + pv
                    m[hd] = m_new
                else:
                    p = jnp.exp2(s - shift_scr[hd, i:i + 1, :]).astype(BF16)
                    pv = _dot(vt_ext[hd, v_rows[hd], c0:c1], p)
                    acc[hd] = pv if acc[hd] is None else acc[hd] + pv
            if c0 == 0:
                out_t = jnp.concatenate(
                    [acc[hd][num_rows[hd], :] / acc[hd][den_row[hd]:den_row[hd] + 1, :]
                     for hd in range(2)], axis=0)
                o_ref[0, 0, i * blk:(i + 1) * blk, :] = out_t.T.astype(o_ref.dtype)
                m, acc = [None, None], [None, None]

    pl.when(bounded)(functools.partial(attend, False))
    pl.when(jnp.logical_not(bounded))(functools.partial(attend, True))


def _moba(q, k, v, chunk):
    bsz, n_pairs, s, _ = q.shape
    spec = pl.BlockSpec((1, 1, s, LANES), lambda b, p: (b, p, 0, 0))
    return pl.pallas_call(
        functools.partial(_moba_kernel, chunk=chunk),
        grid=(bsz, n_pairs),
        in_specs=[spec, spec, spec],
        out_specs=spec,
        out_shape=jax.ShapeDtypeStruct(q.shape, BF16),
        scratch_shapes=[
            pltpu.VMEM((2, LANES, s), BF16),
            pltpu.VMEM((2, s, LANES), BF16),
            pltpu.VMEM((2, LANES, s), BF16),
            pltpu.VMEM((2, s // MOBA_BLOCK, MOBA_BLOCK), F32),
        ],
        compiler_params=pltpu.CompilerParams(
            dimension_semantics=("parallel", "parallel"), vmem_limit_bytes=VMEM_LIMIT_BYTES),
        name="moba",
    )(q, k, v)


def _mix_ffn_kernel(attn_ref, sgu_ref, x_ref, mod_ref, g_attn_ref, w_out_ref, g_post_mix_ref,
                    g_pre_ref, w_gu_ref, w_down_ref, g_post_ref, o_ref, x_scr, act_scr, *,
                    sub, chunk):
    tm = x_ref.shape[1]
    aw = attn_ref.shape[1] * attn_ref.shape[3]
    d_ff = w_down_ref.shape[0]
    gate_m = mod_ref[0, 2:3, :]
    shift = mod_ref[0, 3:4, :]
    scale = mod_ref[0, 4:5, :]
    gate_f = mod_ref[0, 5:6, :]

    subs = [slice(r0, r0 + sub) for r0 in range(0, tm, sub)]
    hs = []
    for rows in subs:
        attn = jnp.concatenate(
            [attn_ref[0, p, rows, :].astype(F32) for p in range(attn_ref.shape[1])], axis=1)
        a = _rms(attn, g_attn_ref[...]).astype(BF16)
        mix = _dot(a, w_out_ref[0:aw, :]) + _dot(sgu_ref[0, rows, :], w_out_ref[aw:, :])
        x = x_ref[0, rows, :] + gate_m * _rms(mix, g_post_mix_ref[...])
        x_scr[rows, :] = x
        hs.append((_rms(x, g_pre_ref[...]) * (1.0 + scale) + shift).astype(BF16))
    for rows, h in zip(subs, hs):
        for c0 in range(0, d_ff, chunk):
            g = _dot(h, w_gu_ref[:, c0:c0 + chunk])
            up = _dot(h, w_gu_ref[:, d_ff + c0:d_ff + c0 + chunk])
            act_scr[rows, c0:c0 + chunk] = (g * jax.nn.sigmoid(g) * up).astype(BF16)
    for rows in subs:
        y = _dot(act_scr[rows, :], w_down_ref[...])
        o_ref[0, rows, :] = x_scr[rows, :] + gate_f * _rms(y, g_post_ref[...])


def _mix_ffn(attn, sgu, x, mod, g_attn, w_out, g_post_mix, g_pre, w_gu, w_down, g_post, tm, sub,
             chunk):
    bsz, s, d = x.shape
    n_pairs = attn.shape[1]
    aw = n_pairs * attn.shape[3]
    d_ff = w_down.shape[0]
    row_spec = lambda w: pl.BlockSpec((1, tm, w), lambda b, t: (b, t, 0))
    full = lambda shape: pl.BlockSpec(shape, lambda b, t: (0,) * len(shape),
                                      pipeline_mode=pl.Buffered(1))
    return pl.pallas_call(
        functools.partial(_mix_ffn_kernel, sub=sub, chunk=chunk),
        grid=(bsz, s // tm),
        in_specs=[
            pl.BlockSpec((1, n_pairs, tm, attn.shape[3]), lambda b, t: (b, 0, t, 0)),
            row_spec(sgu.shape[2]), row_spec(d),
            pl.BlockSpec((1, N_MOD, d), lambda b, t: (b, 0, 0)),
            full((1, aw)), full(w_out.shape), full((1, d)),
            full((1, d)), full(w_gu.shape), full(w_down.shape), full((1, d)),
        ],
        out_specs=row_spec(d),
        out_shape=jax.ShapeDtypeStruct((bsz, s, d), F32),
        scratch_shapes=[pltpu.VMEM((tm, d), F32), pltpu.VMEM((tm, d_ff), BF16)],
        compiler_params=pltpu.CompilerParams(
            dimension_semantics=("parallel", "parallel"), vmem_limit_bytes=VMEM_LIMIT_BYTES),
        name="mix_ffn",
    )(attn, sgu, x, mod, g_attn, w_out, g_post_mix, g_pre, w_gu, w_down, g_post)


def kernel(x, c, w_ada, b_ada, g_pre_mix, g_post_mix, w_in, g_sgu_norm, w_sgu, b_sgu,
           g_attn_out, g_sgu_out, w_out, g_pre_ffn, g_post_ffn, w_gate_up, w_down):
    bsz, s, d = x.shape
    depth = w_ada.shape[0]
    d_ff = w_down.shape[1]
    assert s % MOBA_BLOCK == 0 and s // MOBA_BLOCK <= HEAD_DIM
    assert w_in.shape[2] == 3 * ATTN_WIDTH + 2 * N_SGU_GROUPS * LANES
    tm_in, sub_in, tm_ffn, sub_ffn = 2048, 256, 1024, 256
    ffn_chunk = 256
    moba_chunk = 2 * MOBA_BLOCK
    assert d_ff % ffn_chunk == 0 and s % tm_in == 0 and s % tm_ffn == 0
    n_in_steps = bsz * (s // tm_in)
    assert d % n_in_steps == 0 and d_ff % n_in_steps == 0
    row = lambda a: a.reshape(1, -1)
    for l in range(depth):
        mod = _ada(c, w_ada[l], b_ada[l]).reshape(bsz, N_MOD, d)
        (q, k, v, sgu), (w_out_b, w_gu_b, w_down_b) = _in_proj(
            x, mod, row(g_pre_mix[l]), w_in[l].astype(BF16), row(g_sgu_norm[l]), w_sgu[l],
            b_sgu[l].T, row(g_sgu_out[l]), [w_out[l], w_gate_up[l], w_down[l]], tm_in, sub_in)
        attn = _moba(q, k, v, moba_chunk)
        x = _mix_ffn(attn, sgu, x, mod, row(g_attn_out[l]), w_out_b, row(g_post_mix[l]),
                     row(g_pre_ffn[l]), w_gu_b, w_down_b, row(g_post_ffn[l]), tm_ffn, sub_ffn,
                     ffn_chunk)
    return x
```

```python
import functools
import math

import jax
import jax.numpy as jnp
from jax import lax
from jax.experimental import pallas as pl
from jax.experimental.pallas import tpu as pltpu

N_HEADS = 8
HEAD_DIM = 64
ATTN_WIDTH = N_HEADS * HEAD_DIM
N_SGU_GROUPS = 4
SGU_CHUNK = 128
MOBA_BLOCK = 256
MOBA_TOPK = 3
N_MOD = 6
EPS = 1e-6
NEG = -1e30

LANES = 128
DEN_ROWS = 16
BOUND_HEADROOM = 80.0
BOUND_GAP = 180.0
V_LIMIT = 2.0 ** 30
VMEM_LIMIT_BYTES = 56 * 1024 * 1024
Q_SCALE = HEAD_DIM ** -0.5 * math.log2(math.e)

BF16 = jnp.bfloat16
F32 = jnp.float32


def _rms(x, g):
    return x * lax.rsqrt(jnp.mean(x * x, axis=-1, keepdims=True) + EPS) * g


def _dot(a, b):
    return jnp.dot(a, b, preferred_element_type=F32)


def _ada_kernel(c_ref, w_ref, b_ref, o_ref):
    c = c_ref[...]
    c_act = (c * jax.nn.sigmoid(c)).astype(BF16)
    o_ref[...] = _dot(c_act, w_ref[...].astype(BF16)) + b_ref[...]


def _ada(c, w_ada, b_ada):
    bsz, d = c.shape
    n = w_ada.shape[1]
    tn = d
    return pl.pallas_call(
        _ada_kernel,
        grid=(n // tn,),
        in_specs=[
            pl.BlockSpec((bsz, d), lambda j: (0, 0)),
            pl.BlockSpec((d, tn), lambda j: (0, j)),
            pl.BlockSpec((1, tn), lambda j: (0, j)),
        ],
        out_specs=pl.BlockSpec((bsz, tn), lambda j: (0, j)),
        out_shape=jax.ShapeDtypeStruct((bsz, n), F32),
        compiler_params=pltpu.CompilerParams(
            dimension_semantics=("arbitrary",), vmem_limit_bytes=VMEM_LIMIT_BYTES),
        name="ada",
    )(c, w_ada, b_ada.reshape(1, n))


def _in_proj_kernel(*refs, sub, n_cast):
    (x_ref, mod_ref, g_pre_ref, w_in_ref, g_ln_ref, w_sgu_ref, b_sgu_ref,
     g_sgu_out_ref) = refs[:8]
    cast_in = refs[8:8 + n_cast]
    q_ref, k_ref, v_ref, sgu_ref = refs[8 + n_cast:12 + n_cast]
    cast_out = refs[12 + n_cast:12 + 2 * n_cast]
    sgu_scr = refs[12 + 2 * n_cast]
    for w_ref, o_ref in zip(cast_in, cast_out):
        o_ref[...] = w_ref[...].astype(BF16)

    tm = x_ref.shape[1]
    aw = ATTN_WIDTH
    gd = LANES
    n_chunks = sub // SGU_CHUNK
    shift = mod_ref[0, 0:1, :]
    scale = mod_ref[0, 1:2, :]
    row = lax.broadcasted_iota(jnp.int32, (SGU_CHUNK, SGU_CHUNK), 0)
    col = lax.broadcasted_iota(jnp.int32, (SGU_CHUNK, SGU_CHUNK), 1)
    w_mix = [jnp.where(col <= row, w_sgu_ref[g], 0.0).astype(BF16) for g in range(N_SGU_GROUPS)]

    subs = [slice(r0, r0 + sub) for r0 in range(0, tm, sub)]
    hs = [(_rms(x_ref[0, rows, :], g_pre_ref[...]) * (1.0 + scale) + shift).astype(BF16)
          for rows in subs]
    projs = [_dot(h, w_in_ref[...]) for h in hs]
    for rows, proj in zip(subs, projs):
        for p in range(aw // LANES):
            lanes = slice(p * LANES, (p + 1) * LANES)
            q_ref[0, p, rows, :] = (proj[:, 0:aw][:, lanes] * Q_SCALE).astype(BF16)
            k_ref[0, p, rows, :] = proj[:, aw:2 * aw][:, lanes].astype(BF16)
            v_ref[0, p, rows, :] = proj[:, 2 * aw:3 * aw][:, lanes].astype(BF16)

        u = jax.nn.gelu(proj[:, 3 * aw:3 * aw + N_SGU_GROUPS * gd])
        vg = jax.nn.gelu(proj[:, 3 * aw + N_SGU_GROUPS * gd:])
        for g in range(N_SGU_GROUPS):
            lo = g * gd
            vgg = vg[:, lo:lo + gd]
            d = vgg - jnp.mean(vgg, axis=-1, keepdims=True)
            yn = (d * lax.rsqrt(jnp.mean(d * d, axis=-1, keepdims=True) + EPS)
                  * g_ln_ref[:, lo:lo + gd]).astype(BF16)
            rhs = jnp.concatenate(
                [yn[n * SGU_CHUNK:(n + 1) * SGU_CHUNK, :] for n in range(n_chunks)], axis=1)
            mixed = _dot(w_mix[g], rhs) + b_sgu_ref[:, g:g + 1]
            for n in range(n_chunks):
                r0 = rows.start + n * SGU_CHUNK
                sgu_scr[r0:r0 + SGU_CHUNK, lo:lo + gd] = (
                    u[n * SGU_CHUNK:(n + 1) * SGU_CHUNK, lo:lo + gd]
                    * mixed[:, n * gd:(n + 1) * gd])
        sgu_ref[0, rows, :] = _rms(sgu_scr[rows, :], g_sgu_out_ref[...]).astype(BF16)


def _in_proj(x, mod, g_pre, w_in, g_ln, w_sgu, b_sgu_t, g_sgu_out, cast_ws, tm, sub):
    bsz, s, d = x.shape
    n_t = s // tm
    n_steps = bsz * n_t
    cast_views = [w.reshape(n_steps, w.shape[0] // n_steps, w.shape[1]) for w in cast_ws]
    cast_specs = [pl.BlockSpec((1,) + w.shape[1:], lambda b, t: (b * n_t + t, 0, 0))
                  for w in cast_views]
    in_w = w_in.shape[1]
    aw = ATTN_WIDTH
    sw = in_w - 3 * aw
    sgu_w = sw // 2
    row_spec = lambda w: pl.BlockSpec((1, tm, w), lambda b, t: (b, t, 0))
    full = lambda shape: pl.BlockSpec(shape, lambda b, t: (0,) * len(shape),
                                      pipeline_mode=pl.Buffered(1))
    n_pairs = aw // LANES
    pair_sds = jax.ShapeDtypeStruct((bsz, n_pairs, s, LANES), BF16)
    pair_spec = pl.BlockSpec((1, n_pairs, tm, LANES), lambda b, t: (b, 0, t, 0))
    outs = pl.pallas_call(
        functools.partial(_in_proj_kernel, sub=sub, n_cast=len(cast_ws)),
        grid=(bsz, n_t),
        in_specs=[
            row_spec(d),
            pl.BlockSpec((1, N_MOD, d), lambda b, t: (b, 0, 0)),
            full((1, d)),
            full((d, in_w)),
            full((1, sgu_w)),
            full((N_SGU_GROUPS, SGU_CHUNK, SGU_CHUNK)),
            full((SGU_CHUNK, N_SGU_GROUPS)),
            full((1, sgu_w)),
        ] + cast_specs,
        out_specs=[pair_spec, pair_spec, pair_spec, row_spec(sgu_w)] + cast_specs,
        out_shape=[pair_sds, pair_sds, pair_sds, jax.ShapeDtypeStruct((bsz, s, sgu_w), BF16)]
        + [jax.ShapeDtypeStruct(w.shape, BF16) for w in cast_views],
        scratch_shapes=[pltpu.VMEM((tm, sgu_w), F32)],
        compiler_params=pltpu.CompilerParams(
            dimension_semantics=("parallel", "parallel"), vmem_limit_bytes=VMEM_LIMIT_BYTES),
        name="in_proj",
    )(x, mod, g_pre, w_in, g_ln, w_sgu, b_sgu_t, g_sgu_out, *cast_views)
    return outs[:4], [o.reshape(w.shape) for o, w in zip(outs[4:], cast_ws)]


def _topk_bias_t(g, n_past):
    nrow = lax.broadcasted_iota(jnp.int32, g.shape, 0)
    beaten = jnp.zeros(g.shape, F32)
    for other in range(n_past):
        b = g[other:other + 1, :]
        beaten = beaten + jnp.where(b > g, 1.0, 0.0)
        beaten = beaten + jnp.where(jnp.logical_and(b == g, nrow > other), 1.0, 0.0)
    return jnp.where(nrow < n_past, jnp.where(beaten < MOBA_TOPK, 0.0, NEG), 0.0)


def _moba_kernel(q_ref, k_ref, v_ref, o_ref, qt_ext, k_ext, vt_ext, shift_scr, *, chunk):
    seq = k_ref.shape[2]
    blk = MOBA_BLOCK
    nb = seq // blk
    half = HEAD_DIM
    lane = lax.broadcasted_iota(jnp.int32, (blk, LANES), 1)
    in_head = (lane < half, lane >= half)
    bias_base = (half, 0)

    trow = lax.broadcasted_iota(jnp.int32, (LANES, blk), 0)
    in_head_t = (trow < half, trow >= half)
    v_rows = (slice(0, half + DEN_ROWS), slice(half - DEN_ROWS, LANES))
    num_rows = (slice(0, half), slice(DEN_ROWS, DEN_ROWS + half))
    den_row = (half, 0)

    key_pos = lax.broadcasted_iota(jnp.int32, (blk, blk), 0)
    q_pos = lax.broadcasted_iota(jnp.int32, (blk, blk), 1)
    causal = key_pos <= q_pos

    head_rows = (slice(0, half), slice(half, LANES))
    other_rows = (slice(half, LANES), slice(0, half))
    zeros_half_t = jnp.zeros((half, blk), BF16)
    q_norm2 = ([], [])
    first_logit = ([], [])
    k_bound2 = [None, None]
    chan_lane = lax.broadcasted_iota(jnp.int32, (1, LANES), 1)
    chan_in_head = (chan_lane < half, chan_lane >= half)
    for i in range(nb):
        cols = slice(i * blk, (i + 1) * blk)
        kb = k_ref[0, 0, cols, :].astype(F32)
        chan_max2 = jnp.max(kb * kb, axis=0, keepdims=True)
        for hd in range(2):
            blk_bound2 = jnp.sum(jnp.where(chan_in_head[hd], chan_max2, 0.0), axis=1,
                                 keepdims=True)
            k_bound2[hd] = blk_bound2 if i == 0 else jnp.maximum(k_bound2[hd], blk_bound2)
        v_abs = jnp.max(jnp.abs(v_ref[0, 0, cols, :]), axis=0, keepdims=True).astype(F32)
        v_abs_max = v_abs if i == 0 else jnp.maximum(v_abs_max, v_abs)
        k_first = k_ref[0, 0, i * blk:i * blk + 16, :]
        q_t = q_ref[0, 0, cols, :].astype(F32).T
        for hd in range(2):
            qh = q_t[head_rows[hd], :]
            qh_b = qh.astype(BF16)
            pair = [qh_b, zeros_half_t] if hd == 0 else [zeros_half_t, qh_b]
            q_norm2[hd].append(jnp.sum(qh * qh, axis=0, keepdims=True))
            first_logit[hd].append(_dot(k_first, jnp.concatenate(pair, axis=0))[0:1, :])
            qt_ext[hd, head_rows[hd], cols] = qh_b

    shift_lane = -(-nb // 8) * 8

    def prepare_operands(fold_shift):
        zeros_half = jnp.zeros((half, blk), BF16)
        k_means = []
        for i in range(nb):
            cols = slice(i * blk, (i + 1) * blk)
            kb = k_ref[0, 0, cols, :].astype(F32)
            vb_t = v_ref[0, 0, cols, :].astype(F32).T
            if i > 0:
                rows = -(-i // 8) * 8
                km_pair = jnp.concatenate(
                    k_means + [jnp.zeros((rows - i, LANES), F32)] * (rows > i), axis=0)
                klane = lax.broadcasted_iota(jnp.int32, (rows, LANES), 1)
                km = (jnp.where(klane < half, km_pair, 0.0).astype(BF16),
                      jnp.where(klane >= half, km_pair, 0.0).astype(BF16))
            k_means.append(jnp.sum(kb, axis=0, keepdims=True) * (1.0 / blk))
            for hd in range(2):
                rel = lane - bias_base[hd]
                marks = jnp.where(jnp.logical_or(rel == i, rel == shift_lane), 1.0, 0.0)
                k_ext[hd, cols, :] = jnp.where(in_head[hd], kb, marks).astype(BF16)
                vt_ext[hd, :, cols] = jnp.where(in_head_t[hd], vb_t, 1.0).astype(BF16)
                block_bias = jnp.zeros((shift_lane, blk), F32)
                if i > 0:
                    q_own = qt_ext[hd, head_rows[hd], cols]
                    pair = [q_own, zeros_half] if hd == 0 else [zeros_half, q_own]
                    gate_t = _dot(km[hd], jnp.concatenate(pair, axis=0))
                    block_bias = _topk_bias_t(gate_t, i)
                    if rows < shift_lane:
                        block_bias = jnp.concatenate(
                            [block_bias, jnp.zeros((shift_lane - rows, blk), F32)], axis=0)
                if fold_shift:
                    shift_rows = jnp.broadcast_to(-shift_scr[hd, i:i + 1, :], (8, blk))
                else:
                    shift_rows = jnp.zeros((8, blk), F32)
                bias = jnp.concatenate(
                    [block_bias, shift_rows, jnp.zeros((half - shift_lane - 8, blk), F32)], axis=0)
                qt_ext[hd, other_rows[hd], cols] = bias.astype(BF16)

    items = [(i, max(c1 - chunk, 0), c1)
             for i in range(nb) for c1 in range((i + 1) * blk, 0, -chunk)]

    def scores(item):
        i, c0, c1 = item
        return [_dot(k_ext[hd, c0:c1, :], qt_ext[hd, :, i * blk:(i + 1) * blk])
                for hd in range(2)]

    gap = None
    for hd in range(2):
        upper = jnp.sqrt(jnp.concatenate(q_norm2[hd], axis=0) * k_bound2[hd])
        shift_scr[hd] = upper - BOUND_HEADROOM
        hd_gap = upper - jnp.concatenate(first_logit[hd], axis=0)
        gap = hd_gap if gap is None else jnp.maximum(gap, hd_gap)
    bounded = jnp.logical_and(jnp.max(gap) <= BOUND_GAP, jnp.max(v_abs_max) <= V_LIMIT)

    def attend(exact_max):
        prepare_operands(fold_shift=not exact_max)
        s_next = scores(items[0])
        m = [None, None]
        acc = [None, None]
        for n, (i, c0, c1) in enumerate(items):
            s_cur = s_next
            if n + 1 < len(items):
                s_next = scores(items[n + 1])
            cw = c1 - c0
            for hd in range(2):
                s = s_cur[hd]
                if c1 == (i + 1) * blk:
                    own = jnp.where(causal, s[cw - blk:, :], NEG)
                    s = own if cw == blk else jnp.concatenate([s[:cw - blk, :], own], axis=0)
                if exact_max:
                    m_chunk = jnp.max(jnp.max(s.reshape(cw // 8, 8, blk), axis=0), axis=0,
                                      keepdims=True)
                    m_new = m_chunk if m[hd] is None else jnp.maximum(m[hd], m_chunk)
                    pv = _dot(vt_ext[hd, v_rows[hd], c0:c1], jnp.exp2(s - m_new).astype(BF16))
                    acc[hd] = pv if acc[hd] is None else acc[hd] * jnp.exp2(m[hd] - m_new) + pv
                    m[hd] = m_new
                else:
                    p = jnp.exp2(s).astype(BF16)
                    pv = _dot(vt_ext[hd, v_rows[hd], c0:c1], p)
                    acc[hd] = pv if acc[hd] is None else acc[hd] + pv
            if c0 == 0:
                out_t = jnp.concatenate(
                    [acc[hd][num_rows[hd], :] / acc[hd][den_row[hd]:den_row[hd] + 1, :]
                     for hd in range(2)], axis=0)
                o_ref[0, 0, i * blk:(i + 1) * blk, :] = out_t.T.astype(o_ref.dtype)
                m, acc = [None, None], [None, None]

    pl.when(bounded)(functools.partial(attend, False))
    pl.when(jnp.logical_not(bounded))(functools.partial(attend, True))


def _moba(q, k, v, chunk):
    bsz, n_pairs, s, _ = q.shape
    spec = pl.BlockSpec((1, 1, s, LANES), lambda b, p: (b, p, 0, 0))
    return pl.pallas_call(
        functools.partial(_moba_kernel, chunk=chunk),
        grid=(bsz, n_pairs),
        in_specs=[spec, spec, spec],
        out_specs=spec,
        out_shape=jax.ShapeDtypeStruct(q.shape, BF16),
        scratch_shapes=[
            pltpu.VMEM((2, LANES, s), BF16),
            pltpu.VMEM((2, s, LANES), BF16),
            pltpu.VMEM((2, LANES, s), BF16),
            pltpu.VMEM((2, s // MOBA_BLOCK, MOBA_BLOCK), F32),
        ],
        compiler_params=pltpu.CompilerParams(
            dimension_semantics=("parallel", "parallel"), vmem_limit_bytes=VMEM_LIMIT_BYTES),
        name="moba",
    )(q, k, v)


def _mix_ffn_kernel(attn_ref, sgu_ref, x_ref, mod_ref, g_attn_ref, w_out_ref, g_post_mix_ref,
                    g_pre_ref, w_gu_ref, w_down_ref, g_post_ref, o_ref, x_scr, act_scr, *,
                    sub, chunk):
    tm = x_ref.shape[1]
    aw = attn_ref.shape[1] * attn_ref.shape[3]
    d_ff = w_down_ref.shape[0]
    gate_m = mod_ref[0, 2:3, :]
    shift = mod_ref[0, 3:4, :]
    scale = mod_ref[0, 4:5, :]
    gate_f = mod_ref[0, 5:6, :]

    subs = [slice(r0, r0 + sub) for r0 in range(0, tm, sub)]
    hs = []
    for rows in subs:
        attn = jnp.concatenate(
            [attn_ref[0, p, rows, :].astype(F32) for p in range(attn_ref.shape[1])], axis=1)
        a = _rms(attn, g_attn_ref[...]).astype(BF16)
        mix = _dot(a, w_out_ref[0:aw, :]) + _dot(sgu_ref[0, rows, :], w_out_ref[aw:, :])
        x = x_ref[0, rows, :] + gate_m * _rms(mix, g_post_mix_ref[...])
        x_scr[rows, :] = x
        hs.append((_rms(x, g_pre_ref[...]) * (1.0 + scale) + shift).astype(BF16))
    for rows, h in zip(subs, hs):
        for c0 in range(0, d_ff, chunk):
            g = _dot(h, w_gu_ref[:, c0:c0 + chunk])
            up = _dot(h, w_gu_ref[:, d_ff + c0:d_ff + c0 + chunk])
            act_scr[rows, c0:c0 + chunk] = (g * jax.nn.sigmoid(g) * up).astype(BF16)
    for rows in subs:
        y = _dot(act_scr[rows, :], w_down_ref[...])
        o_ref[0, rows, :] = x_scr[rows, :] + gate_f * _rms(y, g_post_ref[...])


def _mix_ffn(attn, sgu, x, mod, g_attn, w_out, g_post_mix, g_pre, w_gu, w_down, g_post, tm, sub,
             chunk):
    bsz, s, d = x.shape
    n_pairs = attn.shape[1]
    aw = n_pairs * attn.shape[3]
    d_ff = w_down.shape[0]
    row_spec = lambda w: pl.BlockSpec((1, tm, w), lambda b, t: (b, t, 0))
    full = lambda shape: pl.BlockSpec(shape, lambda b, t: (0,) * len(shape),
                                      pipeline_mode=pl.Buffered(1))
    return pl.pallas_call(
        functools.partial(_mix_ffn_kernel, sub=sub, chunk=chunk),
        grid=(bsz, s // tm),
        in_specs=[
            pl.BlockSpec((1, n_pairs, tm, attn.shape[3]), lambda b, t: (b, 0, t, 0)),
            row_spec(sgu.shape[2]), row_spec(d),
            pl.BlockSpec((1, N_MOD, d), lambda b, t: (b, 0, 0)),
            full((1, aw)), full(w_out.shape), full((1, d)),
            full((1, d)), full(w_gu.shape), full(w_down.shape), full((1, d)),
        ],
        out_specs=row_spec(d),
        out_shape=jax.ShapeDtypeStruct((bsz, s, d), F32),
        scratch_shapes=[pltpu.VMEM((tm, d), F32), pltpu.VMEM((tm, d_ff), BF16)],
        compiler_params=pltpu.CompilerParams(
            dimension_semantics=("parallel", "parallel"), vmem_limit_bytes=VMEM_LIMIT_BYTES),
        name="mix_ffn",
    )(attn, sgu, x, mod, g_attn, w_out, g_post_mix, g_pre, w_gu, w_down, g_post)


def kernel(x, c, w_ada, b_ada, g_pre_mix, g_post_mix, w_in, g_sgu_norm, w_sgu, b_sgu,
           g_attn_out, g_sgu_out, w_out, g_pre_ffn, g_post_ffn, w_gate_up, w_down):
    bsz, s, d = x.shape
    depth = w_ada.shape[0]
    d_ff = w_down.shape[1]
    assert s % MOBA_BLOCK == 0 and -(-(s // MOBA_BLOCK) // 8) * 8 + 8 <= HEAD_DIM
    assert w_in.shape[2] == 3 * ATTN_WIDTH + 2 * N_SGU_GROUPS * LANES
    tm_in, sub_in, tm_ffn, sub_ffn = 2048, 256, 1024, 256
    ffn_chunk = 256
    moba_chunk = 2 * MOBA_BLOCK
    assert d_ff % ffn_chunk == 0 and s % tm_in == 0 and s % tm_ffn == 0
    n_in_steps = bsz * (s // tm_in)
    assert d % n_in_steps == 0 and d_ff % n_in_steps == 0
    row = lambda a: a.reshape(1, -1)
    for l in range(depth):
        mod = _ada(c, w_ada[l], b_ada[l]).reshape(bsz, N_MOD, d)
        (q, k, v, sgu), (w_out_b, w_gu_b, w_down_b) = _in_proj(
            x, mod, row(g_pre_mix[l]), w_in[l].astype(BF16), row(g_sgu_norm[l]), w_sgu[l],
            b_sgu[l].T, row(g_sgu_out[l]), [w_out[l], w_gate_up[l], w_down[l]], tm_in, sub_in)
        attn = _moba(q, k, v, moba_chunk)
        x = _mix_ffn(attn, sgu, x, mod, row(g_attn_out[l]), w_out_b, row(g_post_mix[l]),
                     row(g_pre_ffn[l]), w_gu_b, w_down_b, row(g_post_ffn[l]), tm_ffn, sub_ffn,
                     ffn_chunk)
    return x
```

```python
import functools
import math

import jax
import jax.numpy as jnp
from jax import lax
from jax.experimental import pallas as pl
from jax.experimental.pallas import tpu as pltpu

N_HEADS = 8
HEAD_DIM = 64
ATTN_WIDTH = N_HEADS * HEAD_DIM
N_SGU_GROUPS = 4
SGU_CHUNK = 128
MOBA_BLOCK = 256
MOBA_TOPK = 3
N_MOD = 6
EPS = 1e-6
NEG = -1e30

LANES = 128
DEN_ROWS = 16
BOUND_HEADROOM = 80.0
BOUND_GAP = 180.0
V_LIMIT = 2.0 ** 30
VMEM_LIMIT_BYTES = 56 * 1024 * 1024
Q_SCALE = HEAD_DIM ** -0.5 * math.log2(math.e)

BF16 = jnp.bfloat16
F32 = jnp.float32


def _rms(x, g):
    return x * lax.rsqrt(jnp.mean(x * x, axis=-1, keepdims=True) + EPS) * g


def _dot(a, b):
    return jnp.dot(a, b, preferred_element_type=F32)


def _ada_kernel(c_ref, w_ref, b_ref, w_in_ref, o_ref, w_in_bf16_ref):
    c = c_ref[...]
    c_act = (c * jax.nn.sigmoid(c)).astype(BF16)
    o_ref[...] = _dot(c_act, w_ref[...].astype(BF16)) + b_ref[...]
    w_in_bf16_ref[...] = w_in_ref[...].astype(BF16)


def _ada(c, w_ada, b_ada, w_in, n_steps):
    bsz, d = c.shape
    n = w_ada.shape[1]
    tn = n // n_steps
    w_in_view = w_in.reshape(n_steps, w_in.shape[0] // n_steps, w_in.shape[1])
    w_in_spec = pl.BlockSpec((1,) + w_in_view.shape[1:], lambda j: (j, 0, 0))
    mod, w_in_b = pl.pallas_call(
        _ada_kernel,
        grid=(n_steps,),
        in_specs=[
            pl.BlockSpec((bsz, d), lambda j: (0, 0)),
            pl.BlockSpec((d, tn), lambda j: (0, j)),
            pl.BlockSpec((1, tn), lambda j: (0, j)),
            w_in_spec,
        ],
        out_specs=[pl.BlockSpec((bsz, tn), lambda j: (0, j)), w_in_spec],
        out_shape=[jax.ShapeDtypeStruct((bsz, n), F32),
                   jax.ShapeDtypeStruct(w_in_view.shape, BF16)],
        compiler_params=pltpu.CompilerParams(
            dimension_semantics=("arbitrary",), vmem_limit_bytes=VMEM_LIMIT_BYTES),
        name="ada",
    )(c, w_ada, b_ada.reshape(1, n), w_in_view)
    return mod, w_in_b.reshape(w_in.shape)


def _in_proj_kernel(*refs, sub, n_cast):
    (x_ref, mod_ref, g_pre_ref, w_in_ref, g_ln_ref, w_sgu_ref, b_sgu_ref,
     g_sgu_out_ref) = refs[:8]
    cast_in = refs[8:8 + n_cast]
    q_ref, k_ref, v_ref, sgu_ref = refs[8 + n_cast:12 + n_cast]
    cast_out = refs[12 + n_cast:12 + 2 * n_cast]
    sgu_scr = refs[12 + 2 * n_cast]
    for w_ref, o_ref in zip(cast_in, cast_out):
        o_ref[...] = w_ref[...].astype(BF16)

    tm = x_ref.shape[1]
    aw = ATTN_WIDTH
    gd = LANES
    n_chunks = sub // SGU_CHUNK
    shift = mod_ref[0, 0:1, :]
    scale = mod_ref[0, 1:2, :]
    row = lax.broadcasted_iota(jnp.int32, (SGU_CHUNK, SGU_CHUNK), 0)
    col = lax.broadcasted_iota(jnp.int32, (SGU_CHUNK, SGU_CHUNK), 1)
    w_mix = [jnp.where(col <= row, w_sgu_ref[g], 0.0).astype(BF16) for g in range(N_SGU_GROUPS)]

    subs = [slice(r0, r0 + sub) for r0 in range(0, tm, sub)]
    hs = [(_rms(x_ref[0, rows, :], g_pre_ref[...]) * (1.0 + scale) + shift).astype(BF16)
          for rows in subs]
    projs = [_dot(h, w_in_ref[...]) for h in hs]
    for rows, proj in zip(subs, projs):
        for p in range(aw // LANES):
            lanes = slice(p * LANES, (p + 1) * LANES)
            q_ref[0, p, rows, :] = (proj[:, 0:aw][:, lanes] * Q_SCALE).astype(BF16)
            k_ref[0, p, rows, :] = proj[:, aw:2 * aw][:, lanes].astype(BF16)
            v_ref[0, p, rows, :] = proj[:, 2 * aw:3 * aw][:, lanes].astype(BF16)

        u = jax.nn.gelu(proj[:, 3 * aw:3 * aw + N_SGU_GROUPS * gd])
        vg = jax.nn.gelu(proj[:, 3 * aw + N_SGU_GROUPS * gd:])
        for g in range(N_SGU_GROUPS):
            lo = g * gd
            vgg = vg[:, lo:lo + gd]
            d = vgg - jnp.mean(vgg, axis=-1, keepdims=True)
            yn = (d * lax.rsqrt(jnp.mean(d * d, axis=-1, keepdims=True) + EPS)
                  * g_ln_ref[:, lo:lo + gd]).astype(BF16)
            rhs = jnp.concatenate(
                [yn[n * SGU_CHUNK:(n + 1) * SGU_CHUNK, :] for n in range(n_chunks)], axis=1)
            mixed = _dot(w_mix[g], rhs) + b_sgu_ref[:, g:g + 1]
            for n in range(n_chunks):
                r0 = rows.start + n * SGU_CHUNK
                sgu_scr[r0:r0 + SGU_CHUNK, lo:lo + gd] = (
                    u[n * SGU_CHUNK:(n + 1) * SGU_CHUNK, lo:lo + gd]
                    * mixed[:, n * gd:(n + 1) * gd])
        sgu_ref[0, rows, :] = _rms(sgu_scr[rows, :], g_sgu_out_ref[...]).astype(BF16)


def _in_proj(x, mod, g_pre, w_in, g_ln, w_sgu, b_sgu_t, g_sgu_out, cast_ws, tm, sub):
    bsz, s, d = x.shape
    n_t = s // tm
    n_steps = bsz * n_t
    cast_views = [w.reshape(n_steps, w.shape[0] // n_steps, w.shape[1]) for w in cast_ws]
    cast_specs = [pl.BlockSpec((1,) + w.shape[1:], lambda b, t: (b * n_t + t, 0, 0))
                  for w in cast_views]
    in_w = w_in.shape[1]
    aw = ATTN_WIDTH
    sw = in_w - 3 * aw
    sgu_w = sw // 2
    row_spec = lambda w: pl.BlockSpec((1, tm, w), lambda b, t: (b, t, 0))
    full = lambda shape: pl.BlockSpec(shape, lambda b, t: (0,) * len(shape),
                                      pipeline_mode=pl.Buffered(1))
    n_pairs = aw // LANES
    pair_sds = jax.ShapeDtypeStruct((bsz, n_pairs, s, LANES), BF16)
    pair_spec = pl.BlockSpec((1, n_pairs, tm, LANES), lambda b, t: (b, 0, t, 0))
    outs = pl.pallas_call(
        functools.partial(_in_proj_kernel, sub=sub, n_cast=len(cast_ws)),
        grid=(bsz, n_t),
        in_specs=[
            row_spec(d),
            pl.BlockSpec((1, N_MOD, d), lambda b, t: (b, 0, 0)),
            full((1, d)),
            full((d, in_w)),
            full((1, sgu_w)),
            full((N_SGU_GROUPS, SGU_CHUNK, SGU_CHUNK)),
            full((SGU_CHUNK, N_SGU_GROUPS)),
            full((1, sgu_w)),
        ] + cast_specs,
        out_specs=[pair_spec, pair_spec, pair_spec, row_spec(sgu_w)] + cast_specs,
        out_shape=[pair_sds, pair_sds, pair_sds, jax.ShapeDtypeStruct((bsz, s, sgu_w), BF16)]
        + [jax.ShapeDtypeStruct(w.shape, BF16) for w in cast_views],
        scratch_shapes=[pltpu.VMEM((tm, sgu_w), F32)],
        compiler_params=pltpu.CompilerParams(
            dimension_semantics=("parallel", "parallel"), vmem_limit_bytes=VMEM_LIMIT_BYTES),
        name="in_proj",
    )(x, mod, g_pre, w_in, g_ln, w_sgu, b_sgu_t, g_sgu_out, *cast_views)
    return outs[:4], [o.reshape(w.shape) for o, w in zip(outs[4:], cast_ws)]


def _topk_bias_t(g, n_past):
    nrow = lax.broadcasted_iota(jnp.int32, g.shape, 0)
    beaten = jnp.zeros(g.shape, F32)
    for other in range(n_past):
        b = g[other:other + 1, :]
        beaten = beaten + jnp.where(b > g, 1.0, 0.0)
        beaten = beaten + jnp.where(jnp.logical_and(b == g, nrow > other), 1.0, 0.0)
    return jnp.where(nrow < n_past, jnp.where(beaten < MOBA_TOPK, 0.0, NEG), 0.0)


def _moba_kernel(q_ref, k_ref, v_ref, o_ref, qt_ext, k_ext, vt_ext, shift_scr, *, chunk):
    seq = k_ref.shape[2]
    blk = MOBA_BLOCK
    nb = seq // blk
    half = HEAD_DIM
    lane = lax.broadcasted_iota(jnp.int32, (blk, LANES), 1)
    in_head = (lane < half, lane >= half)
    bias_base = (half, 0)

    trow = lax.broadcasted_iota(jnp.int32, (LANES, blk), 0)
    in_head_t = (trow < half, trow >= half)
    v_rows = (slice(0, half + DEN_ROWS), slice(half - DEN_ROWS, LANES))
    num_rows = (slice(0, half), slice(DEN_ROWS, DEN_ROWS + half))
    den_row = (half, 0)

    key_pos = lax.broadcasted_iota(jnp.int32, (blk, blk), 0)
    q_pos = lax.broadcasted_iota(jnp.int32, (blk, blk), 1)
    causal = key_pos <= q_pos

    head_rows = (slice(0, half), slice(half, LANES))
    other_rows = (slice(half, LANES), slice(0, half))
    zeros_half_t = jnp.zeros((half, blk), BF16)
    q_norm2 = ([], [])
    first_logit = ([], [])
    k_bound2 = [None, None]
    chan_lane = lax.broadcasted_iota(jnp.int32, (1, LANES), 1)
    chan_in_head = (chan_lane < half, chan_lane >= half)
    for i in range(nb):
        cols = slice(i * blk, (i + 1) * blk)
        kb = k_ref[0, 0, cols, :].astype(F32)
        chan_max2 = jnp.max(kb * kb, axis=0, keepdims=True)
        for hd in range(2):
            blk_bound2 = jnp.sum(jnp.where(chan_in_head[hd], chan_max2, 0.0), axis=1,
                                 keepdims=True)
            k_bound2[hd] = blk_bound2 if i == 0 else jnp.maximum(k_bound2[hd], blk_bound2)
        v_abs = jnp.max(jnp.abs(v_ref[0, 0, cols, :]), axis=0, keepdims=True).astype(F32)
        v_abs_max = v_abs if i == 0 else jnp.maximum(v_abs_max, v_abs)
        k_first = k_ref[0, 0, i * blk:i * blk + 16, :]
        q_t = q_ref[0, 0, cols, :].astype(F32).T
        for hd in range(2):
            qh = q_t[head_rows[hd], :]
            qh_b = qh.astype(BF16)
            pair = [qh_b, zeros_half_t] if hd == 0 else [zeros_half_t, qh_b]
            q_norm2[hd].append(jnp.sum(qh * qh, axis=0, keepdims=True))
            first_logit[hd].append(_dot(k_first, jnp.concatenate(pair, axis=0))[0:1, :])
            qt_ext[hd, head_rows[hd], cols] = qh_b

    shift_lane = -(-nb // 8) * 8

    def prepare_operands(fold_shift):
        zeros_half = jnp.zeros((half, blk), BF16)
        k_means = []
        for i in range(nb):
            cols = slice(i * blk, (i + 1) * blk)
            kb = k_ref[0, 0, cols, :].astype(F32)
            vb_t = v_ref[0, 0, cols, :].astype(F32).T
            if i > 0:
                rows = -(-i // 8) * 8
                km_pair = jnp.concatenate(
                    k_means + [jnp.zeros((rows - i, LANES), F32)] * (rows > i), axis=0)
                klane = lax.broadcasted_iota(jnp.int32, (rows, LANES), 1)
                km = (jnp.where(klane < half, km_pair, 0.0).astype(BF16),
                      jnp.where(klane >= half, km_pair, 0.0).astype(BF16))
            k_means.append(jnp.sum(kb, axis=0, keepdims=True) * (1.0 / blk))
            for hd in range(2):
                rel = lane - bias_base[hd]
                marks = jnp.where(jnp.logical_or(rel == i, rel == shift_lane), 1.0, 0.0)
                k_ext[hd, cols, :] = jnp.where(in_head[hd], kb, marks).astype(BF16)
                vt_ext[hd, :, cols] = jnp.where(in_head_t[hd], vb_t, 1.0).astype(BF16)
                block_bias = jnp.zeros((shift_lane, blk), F32)
                if i > 0:
                    q_own = qt_ext[hd, head_rows[hd], cols]
                    pair = [q_own, zeros_half] if hd == 0 else [zeros_half, q_own]
                    gate_t = _dot(km[hd], jnp.concatenate(pair, axis=0))
                    block_bias = _topk_bias_t(gate_t, i)
                    if rows < shift_lane:
                        block_bias = jnp.concatenate(
                            [block_bias, jnp.zeros((shift_lane - rows, blk), F32)], axis=0)
                if fold_shift:
                    shift_rows = jnp.broadcast_to(-shift_scr[hd, i:i + 1, :], (8, blk))
                else:
                    shift_rows = jnp.zeros((8, blk), F32)
                bias = jnp.concatenate(
                    [block_bias, shift_rows, jnp.zeros((half - shift_lane - 8, blk), F32)], axis=0)
                qt_ext[hd, other_rows[hd], cols] = bias.astype(BF16)

    items = [(i, max(c1 - chunk, 0), c1)
             for i in range(nb) for c1 in range((i + 1) * blk, 0, -chunk)]

    def scores(item):
        i, c0, c1 = item
        return [_dot(k_ext[hd, c0:c1, :], qt_ext[hd, :, i * blk:(i + 1) * blk])
                for hd in range(2)]

    gap = None
    for hd in range(2):
        upper = jnp.sqrt(jnp.concatenate(q_norm2[hd], axis=0) * k_bound2[hd])
        shift_scr[hd] = upper - BOUND_HEADROOM
        hd_gap = upper - jnp.concatenate(first_logit[hd], axis=0)
        gap = hd_gap if gap is None else jnp.maximum(gap, hd_gap)
    bounded = jnp.logical_and(jnp.max(gap) <= BOUND_GAP, jnp.max(v_abs_max) <= V_LIMIT)

    def attend(exact_max):
        prepare_operands(fold_shift=not exact_max)
        s_next = scores(items[0])
        m = [None, None]
        acc = [None, None]
        for n, (i, c0, c1) in enumerate(items):
            s_cur = s_next
            if n + 1 < len(items):
                s_next = scores(items[n + 1])
            cw = c1 - c0
            for hd in range(2):
                s = s_cur[hd]
                if c1 == (i + 1) * blk:
                    own = jnp.where(causal, s[cw - blk:, :], NEG)
                    s = own if cw == blk else jnp.concatenate([s[:cw - blk, :], own], axis=0)
                if exact_max:
                    m_chunk = jnp.max(jnp.max(s.reshape(cw // 8, 8, blk), axis=0), axis=0,
                                      keepdims=True)
                    m_new = m_chunk if m[hd] is None else jnp.maximum(m[hd], m_chunk)
                    pv = _dot(vt_ext[hd, v_rows[hd], c0:c1], jnp.exp2(s - m_new).astype(BF16))
                    acc[hd] = pv if acc[hd] is None else acc[hd] * jnp.exp2(m[hd] - m_new) + pv
                    m[hd] = m_new
                else:
                    p = jnp.exp2(s).astype(BF16)
                    pv = _dot(vt_ext[hd, v_rows[hd], c0:c1], p)
                    acc[hd] = pv if acc[hd] is None else acc[hd] + pv
            if c0 == 0:
                out_t = jnp.concatenate(
                    [acc[hd][num_rows[hd], :] / acc[hd][den_row[hd]:den_row[hd] + 1, :]
                     for hd in range(2)], axis=0)
                o_ref[0, 0, i * blk:(i + 1) * blk, :] = out_t.T.astype(o_ref.dtype)
                m, acc = [None, None], [None, None]

    pl.when(bounded)(functools.partial(attend, False))
    pl.when(jnp.logical_not(bounded))(functools.partial(attend, True))


def _moba(q, k, v, chunk):
    bsz, n_pairs, s, _ = q.shape
    spec = pl.BlockSpec((1, 1, s, LANES), lambda b, p: (b, p, 0, 0))
    return pl.pallas_call(
        functools.partial(_moba_kernel, chunk=chunk),
        grid=(bsz, n_pairs),
        in_specs=[spec, spec, spec],
        out_specs=spec,
        out_shape=jax.ShapeDtypeStruct(q.shape, BF16),
        scratch_shapes=[
            pltpu.VMEM((2, LANES, s), BF16),
            pltpu.VMEM((2, s, LANES), BF16),
            pltpu.VMEM((2, LANES, s), BF16),
            pltpu.VMEM((2, s // MOBA_BLOCK, MOBA_BLOCK), F32),
        ],
        compiler_params=pltpu.CompilerParams(
            dimension_semantics=("parallel", "parallel"), vmem_limit_bytes=VMEM_LIMIT_BYTES),
        name="moba",
    )(q, k, v)


def _mix_ffn_kernel(attn_ref, sgu_ref, x_ref, mod_ref, g_attn_ref, w_out_ref, g_post_mix_ref,
                    g_pre_ref, w_gu_ref, w_down_ref, g_post_ref, o_ref, x_scr, act_scr, *,
                    sub, chunk):
    tm = x_ref.shape[1]
    aw = attn_ref.shape[1] * attn_ref.shape[3]
    d_ff = w_down_ref.shape[0]
    gate_m = mod_ref[0, 2:3, :]
    shift = mod_ref[0, 3:4, :]
    scale = mod_ref[0, 4:5, :]
    gate_f = mod_ref[0, 5:6, :]

    subs = [slice(r0, r0 + sub) for r0 in range(0, tm, sub)]
    hs = []
    for rows in subs:
        attn = jnp.concatenate(
            [attn_ref[0, p, rows, :].astype(F32) for p in range(attn_ref.shape[1])], axis=1)
        a = _rms(attn, g_attn_ref[...]).astype(BF16)
        mix = _dot(a, w_out_ref[0:aw, :]) + _dot(sgu_ref[0, rows, :], w_out_ref[aw:, :])
        x = x_ref[0, rows, :] + gate_m * _rms(mix, g_post_mix_ref[...])
        x_scr[rows, :] = x
        hs.append((_rms(x, g_pre_ref[...]) * (1.0 + scale) + shift).astype(BF16))
    for rows, h in zip(subs, hs):
        for c0 in range(0, d_ff, chunk):
            g = _dot(h, w_gu_ref[:, c0:c0 + chunk])
            up = _dot(h, w_gu_ref[:, d_ff + c0:d_ff + c0 + chunk])
            act_scr[rows, c0:c0 + chunk] = (g * jax.nn.sigmoid(g) * up).astype(BF16)
    for rows in subs:
        y = _dot(act_scr[rows, :], w_down_ref[...])
        o_ref[0, rows, :] = x_scr[rows, :] + gate_f * _rms(y, g_post_ref[...])


def _mix_ffn(attn, sgu, x, mod, g_attn, w_out, g_post_mix, g_pre, w_gu, w_down, g_post, tm, sub,
             chunk):
    bsz, s, d = x.shape
    n_pairs = attn.shape[1]
    aw = n_pairs * attn.shape[3]
    d_ff = w_down.shape[0]
    row_spec = lambda w: pl.BlockSpec((1, tm, w), lambda b, t: (b, t, 0))
    full = lambda shape: pl.BlockSpec(shape, lambda b, t: (0,) * len(shape),
                                      pipeline_mode=pl.Buffered(1))
    return pl.pallas_call(
        functools.partial(_mix_ffn_kernel, sub=sub, chunk=chunk),
        grid=(bsz, s // tm),
        in_specs=[
            pl.BlockSpec((1, n_pairs, tm, attn.shape[3]), lambda b, t: (b, 0, t, 0)),
            row_spec(sgu.shape[2]), row_spec(d),
            pl.BlockSpec((1, N_MOD, d), lambda b, t: (b, 0, 0)),
            full((1, aw)), full(w_out.shape), full((1, d)),
            full((1, d)), full(w_gu.shape), full(w_down.shape), full((1, d)),
        ],
        out_specs=row_spec(d),
        out_shape=jax.ShapeDtypeStruct((bsz, s, d), F32),
        scratch_shapes=[pltpu.VMEM((tm, d), F32), pltpu.VMEM((tm, d_ff), BF16)],
        compiler_params=pltpu.CompilerParams(
            dimension_semantics=("parallel", "parallel"), vmem_limit_bytes=VMEM_LIMIT_BYTES),
        name="mix_ffn",
    )(attn, sgu, x, mod, g_attn, w_out, g_post_mix, g_pre, w_gu, w_down, g_post)


def kernel(x, c, w_ada, b_ada, g_pre_mix, g_post_mix, w_in, g_sgu_norm, w_sgu, b_sgu,
           g_attn_out, g_sgu_out, w_out, g_pre_ffn, g_post_ffn, w_gate_up, w_down):
    bsz, s, d = x.shape
    depth = w_ada.shape[0]
    d_ff = w_down.shape[1]
    assert s % MOBA_BLOCK == 0 and -(-(s // MOBA_BLOCK) // 8) * 8 + 8 <= HEAD_DIM
    assert w_in.shape[2] == 3 * ATTN_WIDTH + 2 * N_SGU_GROUPS * LANES
    tm_in, sub_in, tm_ffn, sub_ffn = 2048, 256, 1024, 256
    ffn_chunk = 256
    ada_steps = 8
    assert (N_MOD * d) % (ada_steps * LANES) == 0 and d % (ada_steps * 16) == 0
    moba_chunk = 2 * MOBA_BLOCK
    assert d_ff % ffn_chunk == 0 and s % tm_in == 0 and s % tm_ffn == 0
    n_in_steps = bsz * (s // tm_in)
    assert d % n_in_steps == 0 and d_ff % n_in_steps == 0
    row = lambda a: a.reshape(1, -1)
    for l in range(depth):
        mod, w_in_b = _ada(c, w_ada[l], b_ada[l], w_in[l], ada_steps)
        mod = mod.reshape(bsz, N_MOD, d)
        (q, k, v, sgu), (w_out_b, w_gu_b, w_down_b) = _in_proj(
            x, mod, row(g_pre_mix[l]), w_in_b, row(g_sgu_norm[l]), w_sgu[l],
            b_sgu[l].T, row(g_sgu_out[l]), [w_out[l], w_gate_up[l], w_down[l]], tm_in, sub_in)
        attn = _moba(q, k, v, moba_chunk)
        x = _mix_ffn(attn, sgu, x, mod, row(g_attn_out[l]), w_out_b, row(g_post_mix[l]),
                     row(g_pre_ffn[l]), w_gu_b, w_down_b, row(g_post_ffn[l]), tm_ffn, sub_ffn,
                     ffn_chunk)
    return x
```
